```python
import jax, jax.numpy as jnp
from jax import lax
import numpy as np

D_MODEL = 1024
BATCH = 16
SEQ = 4096
DEPTH = 2
DEC_BATCH = 8
DEC_SEQ = 32
PAST_LEN = 4096

CHUNK = 64
POOL_WIDTH = D_MODEL // 4
POOL_GROUPS = 4
POOL_WINDOWS = (2, 4, 8, 16)
POOL_GDIM = POOL_WIDTH // POOL_GROUPS
POOL_HIST = max(POOL_WINDOWS) - 1
GMLP_WIDTH = D_MODEL // 2
GMLP_HEADS = 4
GMLP_HDIM = GMLP_WIDTH // GMLP_HEADS
GMLP_CHUNK = 128
CONV_WIDTH = D_MODEL // 4
CONV_K = 31
CONV_HIST = CONV_K - 1
N_BRANCH = 3
IN_COLS = POOL_WIDTH + 2 * GMLP_WIDTH + 2 * CONV_WIDTH + N_BRANCH * D_MODEL
MEM_LEN = 256
XA_HEADS = 4
XA_HDIM = D_MODEL // XA_HEADS
N_GROUPS = 4
EXP_PER_GROUP = 4
N_EXPERTS = N_GROUPS * EXP_PER_GROUP
TOP_K = 2
D_EXPERT = D_MODEL // 4
EPS = 1e-6

kernel_name = 'hybrid_pool_gmlp_conv_moe_stream_step'


def rmsnorm(x, g):
    xf = x.astype(jnp.float32)
    y = xf * lax.rsqrt(jnp.mean(xf * xf, axis=-1, keepdims=True) + EPS)
    return (y * g.astype(jnp.float32)).astype(x.dtype)


def layernorm(x, g, b):
    xf = x.astype(jnp.float32)
    mu = jnp.mean(xf, axis=-1, keepdims=True)
    xc = xf - mu
    var = jnp.mean(xc * xc, axis=-1, keepdims=True)
    y = xc * lax.rsqrt(var + EPS) * g.astype(jnp.float32) + b.astype(jnp.float32)
    return y.astype(x.dtype)


def pool_mixer(p, hist, pos0, w_pool, s_pool):
    B, T, _ = p.shape
    full = jnp.concatenate([hist, p], axis=1)
    cs = jnp.cumsum(full.astype(jnp.float32), axis=1)
    cs = jnp.pad(cs, ((0, 0), (1, 0), (0, 0)))
    pos = pos0 + jnp.arange(T)
    means = []
    for g, w in enumerate(POOL_WINDOWS):
        csg = cs[:, :, g * POOL_GDIM:(g + 1) * POOL_GDIM]
        win = csg[:, POOL_HIST + 1:POOL_HIST + 1 + T] - csg[:, POOL_HIST + 1 - w:POOL_HIST + 1 - w + T]
        cnt = jnp.minimum(w, pos + 1).astype(jnp.float32)[None, :, None]
        means.append(win / cnt)
    d = (jnp.concatenate(means, axis=-1) - p.astype(jnp.float32)).astype(p.dtype)
    y = jnp.einsum('btgc,gce->btge', d.reshape(B, T, POOL_GROUPS, POOL_GDIM), w_pool)
    y = y.reshape(B, T, D_MODEL) * s_pool
    return y, full[:, -POOL_HIST:]


def gmlp_mixer(z, g_gv, b_gv, w_s, b_s, w_gmlp_out):
    B, T, _ = z.shape
    z = jax.nn.gelu(z)
    u, v = jnp.split(z, 2, axis=-1)
    vn = layernorm(v, g_gv, b_gv)
    nc = -(-T // GMLP_CHUNK)
    tp = nc * GMLP_CHUNK
    vp = jnp.pad(vn, ((0, 0), (0, tp - T), (0, 0))).reshape(B, nc, GMLP_CHUNK, GMLP_HEADS, GMLP_HDIM)
    blk = jnp.arange(GMLP_CHUNK) // CHUNK
    mask = blk[:, None] >= blk[None, :]
    ws = jnp.where(mask[None], w_s, jnp.zeros((), w_s.dtype))
    mixed = jnp.einsum('hij,bnjhc->bnihc', ws, vp) + jnp.transpose(b_s)[None, None, :, :, None]
    mixed = mixed.reshape(B, tp, GMLP_WIDTH)[:, :T]
    return (u * mixed) @ w_gmlp_out, vn


def conv_mixer(c, hist, w_dw, b_dw, g_cln, b_cln, w_conv_out):
    ca, cg = jnp.split(c, 2, axis=-1)
    a = ca * jax.nn.sigmoid(cg)
    full = jnp.concatenate([hist, a], axis=1)
    h = lax.conv_general_dilated(full, w_dw[:, None, :], (1,), 'VALID',
                                 dimension_numbers=('NWC', 'WIO', 'NWC'),
                                 feature_group_count=CONV_WIDTH) + b_dw
    h = jax.nn.silu(layernorm(h, g_cln, b_cln))
    return h @ w_conv_out, full[:, -CONV_HIST:]


def mixer_block(xn, pool_hist, conv_hist, pos0, w_in, b_gate, w_pool, s_pool, g_gv, b_gv, w_s, b_s,
                w_gmlp_out, w_dw, b_dw, g_cln, b_cln, w_conv_out, w_out):
    B, T, _ = xn.shape
    h = xn @ w_in
    o1 = POOL_WIDTH
    o2 = o1 + 2 * GMLP_WIDTH
    o3 = o2 + 2 * CONV_WIDTH
    gates = jax.nn.sigmoid(h[..., o3:].reshape(B, T, N_BRANCH, D_MODEL) + b_gate)
    ya, pool_new = pool_mixer(h[..., :o1], pool_hist, pos0, w_pool, s_pool)
    yb, vn = gmlp_mixer(h[..., o1:o2], g_gv, b_gv, w_s, b_s, w_gmlp_out)
    yc, conv_new = conv_mixer(h[..., o2:o3], conv_hist, w_dw, b_dw, g_cln, b_cln, w_conv_out)
    merged = gates[:, :, 0] * ya + gates[:, :, 1] * yb + gates[:, :, 2] * yc
    return merged @ w_out, pool_new, conv_new, vn


def memory_kv(mem, g_mem, w_xk, w_xv):
    B = mem.shape[0]
    mn = rmsnorm(mem, g_mem)
    k = (mn @ w_xk).reshape(B, MEM_LEN, XA_HEADS, XA_HDIM)
    v = (mn @ w_xv).reshape(B, MEM_LEN, XA_HEADS, XA_HDIM)
    return k, v


def cross_attn(xn, k, v, w_xq, w_xo):
    B, T, _ = xn.shape
    q = (xn @ w_xq).reshape(B, T, XA_HEADS, XA_HDIM)
    s = jnp.einsum('bthd,bmhd->bhtm', q, k).astype(jnp.float32) * (XA_HDIM ** -0.5)
    p = jax.nn.softmax(s, axis=-1).astype(v.dtype)
    o = jnp.einsum('bhtm,bmhd->bthd', p, v).reshape(B, T, D_MODEL)
    return o @ w_xo


def hier_moe(xn, w_rg, b_rg, w_re, b_re, w_eg, w_eu, w_ed):
    B, T, D = xn.shape
    xf = xn.reshape(B * T, D)
    gl = (xf @ w_rg).astype(jnp.float32) + b_rg.astype(jnp.float32)
    gsel = jnp.argmax(gl, axis=-1)
    pg = jnp.take_along_axis(jax.nn.softmax(gl, axis=-1), gsel[:, None], axis=-1)
    el = ((xf @ w_re).astype(jnp.float32) + b_re.astype(jnp.float32)).reshape(-1, N_GROUPS, EXP_PER_GROUP)
    el = jnp.take_along_axis(el, gsel[:, None, None], axis=1)[:, 0]
    tv, ti = lax.top_k(el, TOP_K)
    pe = jax.nn.softmax(tv, axis=-1) * pg
    eid = gsel[:, None] * EXP_PER_GROUP + ti
    comb = jnp.sum(jax.nn.one_hot(eid, N_EXPERTS, dtype=jnp.float32) * pe[..., None], axis=1)
    comb = comb.astype(xn.dtype)
    out = jnp.zeros_like(xf)
    for e in range(N_EXPERTS):
        he = jax.nn.silu(xf @ w_eg[e]) * (xf @ w_eu[e])
        out = out + comb[:, e:e + 1] * (he @ w_ed[e])
    return out.reshape(B, T, D)


def setup_inputs(seed: int = 0) -> dict:
    key = jax.random.key(seed)
    ks = iter(jax.random.split(key, 48))
    f32 = jnp.float32

    def nrm(shape, scale):
        return jax.random.normal(next(ks), shape, f32) * scale

    def gain(shape):
        return 1.0 + 0.02 * jax.random.normal(next(ks), shape, f32)

    L, D = DEPTH, D_MODEL
    return {
        'x_prompt': nrm((BATCH, SEQ, D), 1.0),
        'x_sample': nrm((DEC_BATCH, DEC_SEQ, D), 1.0),
        'mem_prompt': nrm((BATCH, MEM_LEN, D), 1.0),
        'cache_pool': nrm((L, DEC_BATCH, POOL_HIST, POOL_WIDTH), 1.0),
        'cache_conv': nrm((L, DEC_BATCH, CONV_HIST, CONV_WIDTH), 0.5),
        'cache_mem_k': nrm((L, DEC_BATCH, MEM_LEN, XA_HEADS, XA_HDIM), 1.0),
        'cache_mem_v': nrm((L, DEC_BATCH, MEM_LEN, XA_HEADS, XA_HDIM), 1.0),
        'g_mix': gain((L, D)),
        'w_in': nrm((L, D, IN_COLS), D ** -0.5),
        'b_gate': nrm((L, N_BRANCH, D), 0.02),
        'w_pool': nrm((L, POOL_GROUPS, POOL_GDIM, D // POOL_GROUPS), POOL_GDIM ** -0.5),
        's_pool': gain((L, D)),
        'g_gv': gain((L, GMLP_WIDTH)),
        'b_gv': nrm((L, GMLP_WIDTH), 0.02),
        'w_s': nrm((L, GMLP_HEADS, GMLP_CHUNK, GMLP_CHUNK), GMLP_CHUNK ** -0.5),
        'b_s': gain((L, GMLP_HEADS, GMLP_CHUNK)),
        'w_gmlp_out': nrm((L, GMLP_WIDTH, D), GMLP_WIDTH ** -0.5),
        'w_dw': nrm((L, CONV_K, CONV_WIDTH), CONV_K ** -0.5),
        'b_dw': nrm((L, CONV_WIDTH), 0.02),
        'g_cln': gain((L, CONV_WIDTH)),
        'b_cln': nrm((L, CONV_WIDTH), 0.02),
        'w_conv_out': nrm((L, CONV_WIDTH, D), CONV_WIDTH ** -0.5),
        'w_out': nrm((L, D, D), D ** -0.5),
        'g_xa': gain((L, D)),
        'g_mem': gain((L, D)),
        'w_xq': nrm((L, D, D), D ** -0.5),
        'w_xk': nrm((L, D, D), D ** -0.5),
        'w_xv': nrm((L, D, D), D ** -0.5),
        'w_xo': nrm((L, D, D), D ** -0.5),
        'g_ffn': gain((L, D)),
        'w_rg': nrm((L, D, N_GROUPS), D ** -0.5),
        'b_rg': nrm((L, N_GROUPS), 0.01),
        'w_re': nrm((L, D, N_EXPERTS), D ** -0.5),
        'b_re': nrm((L, N_EXPERTS), 0.01),
        'w_eg': nrm((L, N_EXPERTS, D, D_EXPERT), D ** -0.5),
        'w_eu': nrm((L, N_EXPERTS, D, D_EXPERT), D ** -0.5),
        'w_ed': nrm((L, N_EXPERTS, D_EXPERT, D), D_EXPERT ** -0.5),
        'g_final': gain((D,)),
    }


def reference(x_prompt, x_sample, mem_prompt, cache_pool, cache_conv, cache_mem_k, cache_mem_v,
              g_mix, w_in, b_gate, w_pool, s_pool, g_gv, b_gv, w_s, b_s, w_gmlp_out,
              w_dw, b_dw, g_cln, b_cln, w_conv_out, w_out,
              g_xa, g_mem, w_xq, w_xk, w_xv, w_xo,
              g_ffn, w_rg, b_rg, w_re, b_re, w_eg, w_eu, w_ed, g_final):
    xp = x_prompt
    xs = x_sample
    bp = xp.shape[0]
    pool0 = jnp.zeros((bp, POOL_HIST, POOL_WIDTH), xp.dtype)
    conv0 = jnp.zeros((bp, CONV_HIST, CONV_WIDTH), xp.dtype)
    p_pool, p_conv, p_k, p_v = [], [], [], []
    s_pool_rows, s_conv_rows, s_v_rows = [], [], []
    for l in range(DEPTH):
        mix_w = (w_in[l], b_gate[l], w_pool[l], s_pool[l], g_gv[l], b_gv[l], w_s[l], b_s[l],
                 w_gmlp_out[l], w_dw[l], b_dw[l], g_cln[l], b_cln[l], w_conv_out[l], w_out[l])
        moe_w = (w_rg[l], b_rg[l], w_re[l], b_re[l], w_eg[l], w_eu[l], w_ed[l])
        y, ph, ch, _ = mixer_block(rmsnorm(xp, g_mix[l]), pool0, conv0, 0, *mix_w)
        xp = xp + y
        mk, mv = memory_kv(mem_prompt, g_mem[l], w_xk[l], w_xv[l])
        xp = xp + cross_attn(rmsnorm(xp, g_xa[l]), mk, mv, w_xq[l], w_xo[l])
        xp = xp + hier_moe(rmsnorm(xp, g_ffn[l]), *moe_w)
        p_pool.append(ph)
        p_conv.append(ch)
        p_k.append(mk)
        p_v.append(mv)
        y, sh, sc, vn = mixer_block(rmsnorm(xs, g_mix[l]), cache_pool[l], cache_conv[l], PAST_LEN, *mix_w)
        xs = xs + y
        xs = xs + cross_attn(rmsnorm(xs, g_xa[l]), cache_mem_k[l], cache_mem_v[l], w_xq[l], w_xo[l])
        xs = xs + hier_moe(rmsnorm(xs, g_ffn[l]), *moe_w)
        s_pool_rows.append(sh)
        s_conv_rows.append(sc)
        s_v_rows.append(vn)
    y_prompt = rmsnorm(xp, g_final)
    y_sample = rmsnorm(xs, g_final)
    new_pool_prompt = jnp.stack(p_pool, axis=0)
    new_conv_prompt = jnp.stack(p_conv, axis=0)
    new_mem_k_prompt = jnp.stack(p_k, axis=0)
    new_mem_v_prompt = jnp.stack(p_v, axis=0)
    new_pool_sample = jnp.stack(s_pool_rows, axis=0)
    new_conv_sample = jnp.stack(s_conv_rows, axis=0)
    new_gmlp_v_sample = jnp.stack(s_v_rows, axis=0)
    return (y_prompt, y_sample, new_pool_prompt, new_conv_prompt, new_mem_k_prompt, new_mem_v_prompt,
            new_pool_sample, new_conv_sample, new_gmlp_v_sample)
```

```python
import functools

import jax
import jax.numpy as jnp
from jax import lax
from jax.experimental import pallas as pl
from jax.experimental.pallas import tpu as pltpu

F32 = jnp.float32
BF16 = jnp.bfloat16

EPS = 1e-6
D_MODEL = 1024
PAST_LEN = 4096
CHUNK = 64
POOL_WIDTH = 256
POOL_WINDOWS = (2, 4, 8, 16)
POOL_GDIM = 64
POOL_HIST = 15
POOL_HIST_PAD = 16
GMLP_WIDTH = 512
GMLP_HEADS = 4
GMLP_HDIM = 128
GMLP_CHUNK = 128
CONV_WIDTH = 256
CONV_K = 31
CONV_HIST = 30
CONV_HIST_PAD = 32
N_BRANCH = 3
O_POOL = 0
O_GMLP = POOL_WIDTH
O_CONV = O_GMLP + 2 * GMLP_WIDTH
O_GATE = O_CONV + 2 * CONV_WIDTH
IN_COLS = O_GATE + N_BRANCH * D_MODEL
MEM_LEN = 256
XA_HEADS = 4
XA_HDIM = 256
N_GROUPS = 4
EXP_PER_GROUP = 4
N_EXPERTS = 16
D_EXPERT = 256
ROUTER_LANES = 128
LANE_E0 = N_GROUPS

VMEM_LIMIT_BYTES = 56 * 1024 * 1024


def _dot(a, b):
    return jnp.dot(a, b, preferred_element_type=F32)


def _rmsnorm(x, g):
    return x * lax.rsqrt(jnp.mean(x * x, axis=-1, keepdims=True) + EPS) * g


def _layernorm(x, g, b):
    mu = jnp.mean(x, axis=-1, keepdims=True)
    xc = x - mu
    var = jnp.mean(xc * xc, axis=-1, keepdims=True)
    return xc * lax.rsqrt(var + EPS) * g + b


def _sigmoid(x):
    return 1.0 / (1.0 + jnp.exp(-x))


def _const_spec(shape):
    nd = len(shape)
    return pl.BlockSpec(shape, lambda *_: (0,) * nd, pipeline_mode=pl.Buffered(1))


def _mixer_kernel(x_ref, ph_ref, ch_ref, g_ref, win_ref, bg_ref, wp_ref, sp_ref, ggv_ref, bgv_ref,
                  ws_ref, bs_ref, wgo_ref, wdw_ref, bdw_ref, gcl_ref, bcl_ref, wco_ref, wo_ref,
                  *rest, tm, pos0, emit_vn):
    if emit_vn:
        out_ref, pt_ref, at_ref, vn_ref, pbuf, abuf = rest
    else:
        out_ref, pt_ref, at_ref, pbuf, abuf = rest
        vn_ref = None
    t = pl.program_id(1)

    @pl.when(t == 0)
    def _():
        pbuf[0:POOL_HIST_PAD, :] = ph_ref[...]
        abuf[0:CONV_HIST_PAD, :] = ch_ref[...]

    x = x_ref[...]
    xn = _rmsnorm(x, g_ref[...]).astype(BF16)

    p = _dot(xn, win_ref[:, O_POOL:O_POOL + POOL_WIDTH])
    pbuf[POOL_HIST_PAD:POOL_HIST_PAD + tm, :] = p
    pos = (pos0 + t * tm + lax.broadcasted_iota(jnp.int32, (tm, 1), 0) + 1).astype(F32)
    lane = lax.broadcasted_iota(jnp.int32, (tm, 128), 1)

    def prow(k, c):
        return pbuf[POOL_HIST_PAD - k:POOL_HIST_PAD - k + tm, c * 128:(c + 1) * 128]

    means = []
    for c in range(2):
        w_lo, w_hi = POOL_WINDOWS[2 * c], POOL_WINDOWS[2 * c + 1]
        s_lo = prow(0, c)
        for k in range(1, w_lo):
            s_lo = s_lo + prow(k, c)
        s_hi = s_lo
        for k in range(w_lo, w_hi):
            s_hi = s_hi + prow(k, c)
        m_lo = s_lo / jnp.minimum(float(w_lo), pos)
        m_hi = s_hi / jnp.minimum(float(w_hi), pos)
        means.append(jnp.where(lane < POOL_GDIM, m_lo, m_hi))
    d = (jnp.concatenate(means, axis=1) - p).astype(BF16)
    ya = _dot(d, wp_ref[...]) * sp_ref[...]
    merged = _sigmoid(_dot(xn, win_ref[:, O_GATE:O_GATE + D_MODEL]) + bg_ref[0:1, :]) * ya

    z = jax.nn.gelu(_dot(xn, win_ref[:, O_GMLP:O_GMLP + 2 * GMLP_WIDTH]))
    u = z[:, :GMLP_WIDTH]
    vn = _layernorm(z[:, GMLP_WIDTH:], ggv_ref[...], bgv_ref[...])
    if emit_vn:
        vn_ref[...] = vn
    vb = vn.astype(BF16)
    cr = min(GMLP_CHUNK, tm)
    bi = lax.broadcasted_iota(jnp.int32, (cr, cr), 0) // CHUNK
    bj = lax.broadcasted_iota(jnp.int32, (cr, cr), 1) // CHUNK
    ws = [jnp.where(bi >= bj, ws_ref[h, 0:cr, 0:cr], 0.0).astype(BF16) for h in range(GMLP_HEADS)]
    rows = []
    for c0 in range(0, tm, cr):
        heads = []
        for h in range(GMLP_HEADS):
            vh = vb[c0:c0 + cr, h * GMLP_HDIM:(h + 1) * GMLP_HDIM]
            heads.append(_dot(ws[h], vh) + bs_ref[0:cr, h:h + 1])
        rows.append(jnp.concatenate(heads, axis=1))
    mixed = rows[0] if len(rows) == 1 else jnp.concatenate(rows, axis=0)
    yb = _dot((u * mixed).astype(BF16), wgo_ref[...])
    merged = merged + _sigmoid(_dot(xn, win_ref[:, O_GATE + D_MODEL:O_GATE + 2 * D_MODEL]) + bg_ref[1:2, :]) * yb

    cc = _dot(xn, win_ref[:, O_CONV:O_CONV + 2 * CONV_WIDTH])
    a = cc[:, :CONV_WIDTH] * _sigmoid(cc[:, CONV_WIDTH:])
    abuf[CONV_HIST_PAD:CONV_HIST_PAD + tm, :] = a
    rc = min(64, tm)
    base = CONV_HIST_PAD - CONV_HIST
    hs = []
    for r0 in range(0, tm, rc):
        acc = abuf[base + r0:base + r0 + rc, :] * wdw_ref[0:1, :]
        for k in range(1, CONV_K):
            acc = acc + abuf[base + r0 + k:base + r0 + k + rc, :] * wdw_ref[k:k + 1, :]
        hln = _layernorm(acc + bdw_ref[...], gcl_ref[...], bcl_ref[...])
        hs.append((hln * _sigmoid(hln)).astype(BF16))
    hc = hs[0] if len(hs) == 1 else jnp.concatenate(hs, axis=0)
    yc = _dot(hc, wco_ref[...])
    merged = merged + _sigmoid(_dot(xn, win_ref[:, O_GATE + 2 * D_MODEL:O_GATE + 3 * D_MODEL]) + bg_ref[2:3, :]) * yc

    out_ref[...] = x + _dot(merged.astype(BF16), wo_ref[...])

    p_tail = pbuf[tm:tm + POOL_HIST_PAD, :]
    a_tail = abuf[tm:tm + CONV_HIST_PAD, :]
    pbuf[0:POOL_HIST_PAD, :] = p_tail
    abuf[0:CONV_HIST_PAD, :] = a_tail
    pt_ref[...] = p_tail
    at_ref[...] = a_tail


def _mixer(x, pool_hist, conv_hist, w, *, tm, pos0, emit_vn):
    B, T, D = x.shape
    assert T % tm == 0 and tm >= CONV_HIST_PAD and (tm % GMLP_CHUNK == 0 or tm == T)
    tok = pl.BlockSpec((None, tm, D), lambda b, t: (b, t, 0))
    per_b = lambda r, c: pl.BlockSpec((None, r, c), lambda b, t: (b, 0, 0))
    weights = [w['g_mix'], w['w_in'], w['b_gate'], w['w_pool_bd'], w['s_pool'], w['g_gv'], w['b_gv'],
               w['w_s'], w['b_s_t'], w['w_gmlp_out'], w['w_dw'], w['b_dw'], w['g_cln'], w['b_cln'],
               w['w_conv_out'], w['w_out']]
    out_shape = [jax.ShapeDtypeStruct((B, T, D), F32),
                 jax.ShapeDtypeStruct((B, POOL_HIST_PAD, POOL_WIDTH), F32),
                 jax.ShapeDtypeStruct((B, CONV_HIST_PAD, CONV_WIDTH), F32)]
    out_specs = [tok, per_b(POOL_HIST_PAD, POOL_WIDTH), per_b(CONV_HIST_PAD, CONV_WIDTH)]
    if emit_vn:
        out_shape.append(jax.ShapeDtypeStruct((B, T, GMLP_WIDTH), F32))
        out_specs.append(pl.BlockSpec((None, tm, GMLP_WIDTH), lambda b, t: (b, t, 0)))
    return pl.pallas_call(
        functools.partial(_mixer_kernel, tm=tm, pos0=pos0, emit_vn=emit_vn),
        grid=(B, T // tm),
        in_specs=[tok, per_b(POOL_HIST_PAD, POOL_WIDTH), per_b(CONV_HIST_PAD, CONV_WIDTH)]
                 + [_const_spec(a.shape) for a in weights],
        out_specs=out_specs,
        out_shape=out_shape,
        scratch_shapes=[pltpu.VMEM((POOL_HIST_PAD + tm, POOL_WIDTH), F32),
                        pltpu.VMEM((CONV_HIST_PAD + tm, CONV_WIDTH), F32)],
        compiler_params=pltpu.CompilerParams(dimension_semantics=("arbitrary", "arbitrary"),
                                             vmem_limit_bytes=VMEM_LIMIT_BYTES),
        name="mixer",
    )(x, pool_hist, conv_hist, *weights)


def _kv_kernel(m_ref, g_ref, wk_ref, wv_ref, k_ref, v_ref):
    mn = _rmsnorm(m_ref[...], g_ref[...]).astype(BF16)
    k_ref[...] = _dot(mn, wk_ref[...])
    v_ref[...] = _dot(mn, wv_ref[...])


def _memory_kv(mem, w, *, tm):
    B, M, D = mem.shape
    rows = mem.reshape(B * M, D)
    blk = pl.BlockSpec((tm, D), lambda i: (i, 0))
    k, v = pl.pallas_call(
        _kv_kernel,
        grid=(B * M // tm,),
        in_specs=[blk, _const_spec((1, D)), _const_spec((D, D)), _const_spec((D, D))],
        out_specs=[blk, blk],
        out_shape=[jax.ShapeDtypeStruct((B * M, D), F32)] * 2,
        compiler_params=pltpu.CompilerParams(dimension_semantics=("arbitrary",),
                                             vmem_limit_bytes=VMEM_LIMIT_BYTES),
        name="memory_kv",
    )(rows, w['g_mem'], w['w_xk'], w['w_xv'])
    return k.reshape(B, M, D), v.reshape(B, M, D)


def _xattn_kernel(x_ref, k_ref, v_ref, g_ref, wq_ref, wo_ref, out_ref, kb, vb):
    @pl.when(pl.program_id(1) == 0)
    def _():
        kb[...] = k_ref[...].astype(BF16)
        vb[...] = v_ref[...].astype(BF16)

    x = x_ref[...]
    xn = _rmsnorm(x, g_ref[...]).astype(BF16)
    q = (_dot(xn, wq_ref[...]) * (XA_HDIM ** -0.5)).astype(BF16)
    heads = []
    for h in range(XA_HEADS):
        sl = slice(h * XA_HDIM, (h + 1) * XA_HDIM)
        s = lax.dot_general(q[:, sl], kb[:, sl], (((1,), (1,)), ((), ())), preferred_element_type=F32)
        e = jnp.exp(s - jnp.max(s, axis=-1, keepdims=True))
        pr = (e / jnp.sum(e, axis=-1, keepdims=True)).astype(BF16)
        heads.append(_dot(pr, vb[:, sl]).astype(BF16))
    o = jnp.concatenate(heads, axis=1)
    out_ref[...] = x + _dot(o, wo_ref[...])


def _cross_attn(x, k, v, w, *, tm):
    B, T, D = x.shape
    tok = pl.BlockSpec((None, tm, D), lambda b, t: (b, t, 0))
    mem = pl.BlockSpec((None, MEM_LEN, D), lambda b, t: (b, 0, 0))
    return pl.pallas_call(
        _xattn_kernel,
        grid=(B, T // tm),
        in_specs=[tok, mem, mem, _const_spec((1, D)), _const_spec((D, D)), _const_spec((D, D))],
        out_specs=tok,
        out_shape=jax.ShapeDtypeStruct((B, T, D), F32),
        scratch_shapes=[pltpu.VMEM((MEM_LEN, D), BF16), pltpu.VMEM((MEM_LEN, D), BF16)],
        compiler_params=pltpu.CompilerParams(dimension_semantics=("arbitrary", "arbitrary"),
                                             vmem_limit_bytes=VMEM_LIMIT_BYTES),
        name="cross_attn",
    )(x, k, v, w['g_xa'], w['w_xq'], w['w_xo'])


def _route(xn, wr_hi_ref, wr_lo_ref, br_ref):
    x_hi = xn.astype(BF16)
    x_lo = (xn - x_hi.astype(F32)).astype(BF16)
    logits = (_dot(x_hi, wr_hi_ref[...]) + _dot(x_lo, wr_hi_ref[...]) + _dot(x_hi, wr_lo_ref[...])
              + br_ref[...])
    lane = lax.broadcasted_iota(jnp.int32, logits.shape, 1)
    neg = jnp.float32(-jnp.inf)
    big = jnp.int32(ROUTER_LANES)

    def first_argmax(vals):
        m = jnp.max(vals, axis=-1, keepdims=True)
        return m, jnp.min(jnp.where(vals == m, lane, big), axis=-1, keepdims=True)

    gl = jnp.where(lane < N_GROUPS, logits, neg)
    gmax, gsel = first_argmax(gl)
    pg = 1.0 / jnp.sum(jnp.exp(gl - gmax), axis=-1, keepdims=True)
    e_lo = LANE_E0 + gsel * EXP_PER_GROUP
    el = jnp.where((lane >= e_lo) & (lane < e_lo + EXP_PER_GROUP), logits, neg)
    v1, i1 = first_argmax(el)
    v2, i2 = first_argmax(jnp.where(lane == i1, neg, el))
    r = jnp.exp(v2 - v1)
    pe1 = pg / (1.0 + r)
    pe2 = pg * r / (1.0 + r)
    return jnp.where(lane == i1, pe1, 0.0) + jnp.where(lane == i2, pe2, 0.0)


def _moe_dense_kernel(x_ref, g_ref, wrh_ref, wrl_ref, br_ref, wg_ref, wu_ref, wd_ref, gf_ref,
                      out_ref, xnb, comb, acc, *, final_norm):
    e = pl.program_id(1)

    @pl.when(e == 0)
    def _():
        xn = _rmsnorm(x_ref[...], g_ref[...])
        xnb[...] = xn.astype(BF16)
        comb[...] = _route(xn, wrh_ref, wrl_ref, br_ref)
        acc[...] = jnp.zeros_like(acc)

    lane = lax.broadcasted_iota(jnp.int32, comb.shape, 1)
    c = jnp.sum(jnp.where(lane == e + LANE_E0, comb[...], 0.0), axis=-1, keepdims=True)
    xb = xnb[...]
    hg = _dot(xb, wg_ref[...])
    he = hg * _sigmoid(hg) * _dot(xb, wu_ref[...])
    acc[...] += c * _dot(he.astype(BF16), wd_ref[...])

    @pl.when(e == N_EXPERTS - 1)
    def _():
        y = x_ref[...] + acc[...]
        out_ref[...] = _rmsnorm(y, gf_ref[...]) if final_norm else y


def _moe_dense(x, w, g_final, *, tm, final_norm):
    B, T, D = x.shape
    rows = x.reshape(B * T, D)
    tok = pl.BlockSpec((tm, D), lambda i, e: (i, 0))
    const = lambda shape: pl.BlockSpec(shape, lambda i, e: (0,) * len(shape), pipeline_mode=pl.Buffered(1))
    out = pl.pallas_call(
        functools.partial(_moe_dense_kernel, final_norm=final_norm),
        grid=(B * T // tm, N_EXPERTS),
        in_specs=[tok, const((1, D)), const((D, ROUTER_LANES)), const((D, ROUTER_LANES)),
                  const((1, ROUTER_LANES)),
                  pl.BlockSpec((None, D, D_EXPERT), lambda i, e: (e, 0, 0)),
                  pl.BlockSpec((None, D, D_EXPERT), lambda i, e: (e, 0, 0)),
                  pl.BlockSpec((None, D_EXPERT, D), lambda i, e: (e, 0, 0)),
                  const((1, D))],
        out_specs=tok,
        out_shape=jax.ShapeDtypeStruct((B * T, D), F32),
        scratch_shapes=[pltpu.VMEM((tm, D), BF16), pltpu.VMEM((tm, ROUTER_LANES), F32),
                        pltpu.VMEM((tm, D), F32)],
        compiler_params=pltpu.CompilerParams(dimension_semantics=("arbitrary", "arbitrary"),
                                             vmem_limit_bytes=VMEM_LIMIT_BYTES),
        name="moe_dense",
    )(rows, w['g_ffn'], w['wr_hi'], w['wr_lo'], w['b_r'], w['w_eg'], w['w_eu'], w['w_ed'], g_final)
    return out.reshape(B, T, D)


def _layer_weights(l, g_mix, w_in, b_gate, w_pool, s_pool, g_gv, b_gv, w_s, b_s, w_gmlp_out,
                   w_dw, b_dw, g_cln, b_cln, w_conv_out, w_out, g_xa, g_mem, w_xq, w_xk, w_xv, w_xo,
                   g_ffn, w_rg, b_rg, w_re, b_re, w_eg, w_eu, w_ed):
    row = lambda a: a[l].reshape(1, -1)
    wp = jnp.zeros((POOL_WIDTH, D_MODEL), F32)
    gout = D_MODEL // len(POOL_WINDOWS)
    for g in range(len(POOL_WINDOWS)):
        wp = wp.at[g * POOL_GDIM:(g + 1) * POOL_GDIM, g * gout:(g + 1) * gout].set(w_pool[l, g])
    wr = jnp.zeros((D_MODEL, ROUTER_LANES), F32)
    wr = wr.at[:, :N_GROUPS].set(w_rg[l]).at[:, LANE_E0:LANE_E0 + N_EXPERTS].set(w_re[l])
    wr_hi = wr.astype(BF16)
    br = jnp.zeros((1, ROUTER_LANES), F32)
    br = br.at[0, :N_GROUPS].set(b_rg[l]).at[0, LANE_E0:LANE_E0 + N_EXPERTS].set(b_re[l])
    return {
        'g_mix': row(g_mix), 'w_in': w_in[l].astype(BF16), 'b_gate': b_gate[l],
        'w_pool_bd': wp.astype(BF16), 's_pool': row(s_pool), 'g_gv': row(g_gv), 'b_gv': row(b_gv),
        'w_s': w_s[l], 'b_s_t': jnp.transpose(b_s[l]), 'w_gmlp_out': w_gmlp_out[l].astype(BF16),
        'w_dw': w_dw[l], 'b_dw': row(b_dw), 'g_cln': row(g_cln), 'b_cln': row(b_cln),
        'w_conv_out': w_conv_out[l].astype(BF16), 'w_out': w_out[l].astype(BF16),
        'g_xa': row(g_xa), 'g_mem': row(g_mem), 'w_xq': w_xq[l].astype(BF16),
        'w_xk': w_xk[l].astype(BF16), 'w_xv': w_xv[l].astype(BF16), 'w_xo': w_xo[l].astype(BF16),
        'g_ffn': row(g_ffn), 'wr_hi': wr_hi, 'wr_lo': (wr - wr_hi.astype(F32)).astype(BF16), 'b_r': br,
        'w_eg': w_eg[l].astype(BF16), 'w_eu': w_eu[l].astype(BF16), 'w_ed': w_ed[l].astype(BF16),
    }


def _pad_hist(h, rows):
    return jnp.pad(h, ((0, 0), (rows - h.shape[1], 0), (0, 0)))


def _token_tile(t):
    return min(512, t)


def kernel(x_prompt, x_sample, mem_prompt, cache_pool, cache_conv, cache_mem_k, cache_mem_v, g_mix, w_in, b_gate, w_pool, s_pool, g_gv, b_gv, w_s, b_s, w_gmlp_out, w_dw, b_dw, g_cln, b_cln, w_conv_out, w_out, g_xa, g_mem, w_xq, w_xk, w_xv, w_xo, g_ffn, w_rg, b_rg, w_re, b_re, w_eg, w_eu, w_ed, g_final):
    depth = w_in.shape[0]
    bp, tp, _ = x_prompt.shape
    bs, ts, _ = x_sample.shape
    tmp, tms = _token_tile(tp), _token_tile(ts)
    gf = g_final.reshape(1, -1)
    xp, xs = x_prompt, x_sample
    pool0 = jnp.zeros((bp, POOL_HIST_PAD, POOL_WIDTH), F32)
    conv0 = jnp.zeros((bp, CONV_HIST_PAD, CONV_WIDTH), F32)
    outs = {k: [] for k in ('pp', 'pc', 'pk', 'pv', 'sp', 'sc', 'sv')}
    for l in range(depth):
        w = _layer_weights(l, g_mix, w_in, b_gate, w_pool, s_pool, g_gv, b_gv, w_s, b_s, w_gmlp_out,
                           w_dw, b_dw, g_cln, b_cln, w_conv_out, w_out, g_xa, g_mem, w_xq, w_xk, w_xv,
                           w_xo, g_ffn, w_rg, b_rg, w_re, b_re, w_eg, w_eu, w_ed)
        last = l == depth - 1
        xp, ph, ch = _mixer(xp, pool0, conv0, w, tm=tmp, pos0=0, emit_vn=False)
        mk, mv = _memory_kv(mem_prompt, w, tm=_token_tile(bp * MEM_LEN))
        xp = _cross_attn(xp, mk, mv, w, tm=tmp)
        xp = _moe_dense(xp, w, gf, tm=tmp, final_norm=last)
        outs['pp'].append(ph[:, POOL_HIST_PAD - POOL_HIST:])
        outs['pc'].append(ch[:, CONV_HIST_PAD - CONV_HIST:])
        outs['pk'].append(mk.reshape(bp, MEM_LEN, XA_HEADS, XA_HDIM))
        outs['pv'].append(mv.reshape(bp, MEM_LEN, XA_HEADS, XA_HDIM))
        xs, sh, sc, vn = _mixer(xs, _pad_hist(cache_pool[l], POOL_HIST_PAD),
                                _pad_hist(cache_conv[l], CONV_HIST_PAD), w,
                                tm=tms, pos0=PAST_LEN, emit_vn=True)
        xs = _cross_attn(xs, cache_mem_k[l].reshape(bs, MEM_LEN, D_MODEL),
                         cache_mem_v[l].reshape(bs, MEM_LEN, D_MODEL), w, tm=tms)
        xs = _moe_dense(xs, w, gf, tm=_token_tile(bs * ts), final_norm=last)
        outs['sp'].append(sh[:, POOL_HIST_PAD - POOL_HIST:])
        outs['sc'].append(sc[:, CONV_HIST_PAD - CONV_HIST:])
        outs['sv'].append(vn)
    st = lambda k: jnp.stack(outs[k], axis=0)
    return (xp, xs, st('pp'), st('pc'), st('pk'), st('pv'), st('sp'), st('sc'), st('sv'))
```

```python
import functools

import jax
import jax.numpy as jnp
from jax import lax
from jax.experimental import pallas as pl
from jax.experimental.pallas import tpu as pltpu

F32 = jnp.float32
BF16 = jnp.bfloat16

EPS = 1e-6
D_MODEL = 1024
PAST_LEN = 4096
CHUNK = 64
POOL_WIDTH = 256
POOL_WINDOWS = (2, 4, 8, 16)
POOL_GDIM = 64
POOL_HIST = 15
POOL_HIST_PAD = 16
GMLP_WIDTH = 512
GMLP_HEADS = 4
GMLP_HDIM = 128
GMLP_CHUNK = 128
CONV_WIDTH = 256
CONV_K = 31
CONV_HIST = 30
CONV_HIST_PAD = 32
N_BRANCH = 3
O_POOL = 0
O_GMLP = POOL_WIDTH
O_CONV = O_GMLP + 2 * GMLP_WIDTH
O_GATE = O_CONV + 2 * CONV_WIDTH
IN_COLS = O_GATE + N_BRANCH * D_MODEL
MEM_LEN = 256
XA_HEADS = 4
XA_HDIM = 256
N_GROUPS = 4
EXP_PER_GROUP = 4
N_EXPERTS = 16
D_EXPERT = 256
ROUTER_LANES = 128
PAIRS_PER_GROUP = 6
N_BUCKETS = N_GROUPS * PAIRS_PER_GROUP
PERMUTE_ROWS_PER_STEP = 512
MOE_SORTED_TILE = 256
MOE_SORTED_MIN_ROWS = 1024
LANE_E0 = N_GROUPS

VMEM_LIMIT_BYTES = 56 * 1024 * 1024


def _dot(a, b):
    return jnp.dot(a, b, preferred_element_type=F32)


def _rmsnorm(x, g):
    return x * lax.rsqrt(jnp.mean(x * x, axis=-1, keepdims=True) + EPS) * g


def _layernorm(x, g, b):
    mu = jnp.mean(x, axis=-1, keepdims=True)
    xc = x - mu
    var = jnp.mean(xc * xc, axis=-1, keepdims=True)
    return xc * lax.rsqrt(var + EPS) * g + b


def _sigmoid(x):
    return 1.0 / (1.0 + jnp.exp(-x))


def _const_spec(shape):
    nd = len(shape)
    return pl.BlockSpec(shape, lambda *_: (0,) * nd, pipeline_mode=pl.Buffered(1))


def _mixer_kernel(x_ref, ph_ref, ch_ref, g_ref, win_ref, bg_ref, wp_ref, sp_ref, ggv_ref, bgv_ref,
                  ws_ref, bs_ref, wgo_ref, wdw_ref, bdw_ref, gcl_ref, bcl_ref, wco_ref, wo_ref,
                  *rest, tm, pos0, emit_vn):
    if emit_vn:
        out_ref, pt_ref, at_ref, vn_ref, pbuf, abuf = rest
    else:
        out_ref, pt_ref, at_ref, pbuf, abuf = rest
        vn_ref = None
    t = pl.program_id(1)

    @pl.when(t == 0)
    def _():
        pbuf[0:POOL_HIST_PAD, :] = ph_ref[...]
        abuf[0:CONV_HIST_PAD, :] = ch_ref[...]

    x = x_ref[...]
    xn = _rmsnorm(x, g_ref[...]).astype(BF16)

    p = _dot(xn, win_ref[:, O_POOL:O_POOL + POOL_WIDTH])
    pbuf[POOL_HIST_PAD:POOL_HIST_PAD + tm, :] = p
    pos = (pos0 + t * tm + lax.broadcasted_iota(jnp.int32, (tm, 1), 0) + 1).astype(F32)
    lane = lax.broadcasted_iota(jnp.int32, (tm, 128), 1)

    def prow(k, c):
        return pbuf[POOL_HIST_PAD - k:POOL_HIST_PAD - k + tm, c * 128:(c + 1) * 128]

    means = []
    for c in range(2):
        w_lo, w_hi = POOL_WINDOWS[2 * c], POOL_WINDOWS[2 * c + 1]
        s_lo = prow(0, c)
        for k in range(1, w_lo):
            s_lo = s_lo + prow(k, c)
        s_hi = s_lo
        for k in range(w_lo, w_hi):
            s_hi = s_hi + prow(k, c)
        m_lo = s_lo / jnp.minimum(float(w_lo), pos)
        m_hi = s_hi / jnp.minimum(float(w_hi), pos)
        means.append(jnp.where(lane < POOL_GDIM, m_lo, m_hi))
    d = (jnp.concatenate(means, axis=1) - p).astype(BF16)
    ya = _dot(d, wp_ref[...]) * sp_ref[...]
    merged = _sigmoid(_dot(xn, win_ref[:, O_GATE:O_GATE + D_MODEL]) + bg_ref[0:1, :]) * ya

    z = jax.nn.gelu(_dot(xn, win_ref[:, O_GMLP:O_GMLP + 2 * GMLP_WIDTH]))
    u = z[:, :GMLP_WIDTH]
    vn = _layernorm(z[:, GMLP_WIDTH:], ggv_ref[...], bgv_ref[...])
    if emit_vn:
        vn_ref[...] = vn
    vb = vn.astype(BF16)
    cr = min(GMLP_CHUNK, tm)
    bi = lax.broadcasted_iota(jnp.int32, (cr, cr), 0) // CHUNK
    bj = lax.broadcasted_iota(jnp.int32, (cr, cr), 1) // CHUNK
    ws = [jnp.where(bi >= bj, ws_ref[h, 0:cr, 0:cr], 0.0).astype(BF16) for h in range(GMLP_HEADS)]
    rows = []
    for c0 in range(0, tm, cr):
        heads = []
        for h in range(GMLP_HEADS):
            vh = vb[c0:c0 + cr, h * GMLP_HDIM:(h + 1) * GMLP_HDIM]
            heads.append(_dot(ws[h], vh) + bs_ref[0:cr, h:h + 1])
        rows.append(jnp.concatenate(heads, axis=1))
    mixed = rows[0] if len(rows) == 1 else jnp.concatenate(rows, axis=0)
    yb = _dot((u * mixed).astype(BF16), wgo_ref[...])
    merged = merged + _sigmoid(_dot(xn, win_ref[:, O_GATE + D_MODEL:O_GATE + 2 * D_MODEL]) + bg_ref[1:2, :]) * yb

    cc = _dot(xn, win_ref[:, O_CONV:O_CONV + 2 * CONV_WIDTH])
    a = cc[:, :CONV_WIDTH] * _sigmoid(cc[:, CONV_WIDTH:])
    abuf[CONV_HIST_PAD:CONV_HIST_PAD + tm, :] = a
    rc = min(64, tm)
    base = CONV_HIST_PAD - CONV_HIST
    hs = []
    for r0 in range(0, tm, rc):
        acc = abuf[base + r0:base + r0 + rc, :] * wdw_ref[0:1, :]
        for k in range(1, CONV_K):
            acc = acc + abuf[base + r0 + k:base + r0 + k + rc, :] * wdw_ref[k:k + 1, :]
        hln = _layernorm(acc + bdw_ref[...], gcl_ref[...], bcl_ref[...])
        hs.append((hln * _sigmoid(hln)).astype(BF16))
    hc = hs[0] if len(hs) == 1 else jnp.concatenate(hs, axis=0)
    yc = _dot(hc, wco_ref[...])
    merged = merged + _sigmoid(_dot(xn, win_ref[:, O_GATE + 2 * D_MODEL:O_GATE + 3 * D_MODEL]) + bg_ref[2:3, :]) * yc

    out_ref[...] = x + _dot(merged.astype(BF16), wo_ref[...])

    p_tail = pbuf[tm:tm + POOL_HIST_PAD, :]
    a_tail = abuf[tm:tm + CONV_HIST_PAD, :]
    pbuf[0:POOL_HIST_PAD, :] = p_tail
    abuf[0:CONV_HIST_PAD, :] = a_tail
    pt_ref[...] = p_tail
    at_ref[...] = a_tail


def _mixer(x, pool_hist, conv_hist, w, *, tm, pos0, emit_vn):
    B, T, D = x.shape
    assert T % tm == 0 and tm >= CONV_HIST_PAD and (tm % GMLP_CHUNK == 0 or tm == T)
    tok = pl.BlockSpec((None, tm, D), lambda b, t: (b, t, 0))
    per_b = lambda r, c: pl.BlockSpec((None, r, c), lambda b, t: (b, 0, 0))
    weights = [w['g_mix'], w['w_in'], w['b_gate'], w['w_pool_bd'], w['s_pool'], w['g_gv'], w['b_gv'],
               w['w_s'], w['b_s_t'], w['w_gmlp_out'], w['w_dw'], w['b_dw'], w['g_cln'], w['b_cln'],
               w['w_conv_out'], w['w_out']]
    out_shape = [jax.ShapeDtypeStruct((B, T, D), F32),
                 jax.ShapeDtypeStruct((B, POOL_HIST_PAD, POOL_WIDTH), F32),
                 jax.ShapeDtypeStruct((B, CONV_HIST_PAD, CONV_WIDTH), F32)]
    out_specs = [tok, per_b(POOL_HIST_PAD, POOL_WIDTH), per_b(CONV_HIST_PAD, CONV_WIDTH)]
    if emit_vn:
        out_shape.append(jax.ShapeDtypeStruct((B, T, GMLP_WIDTH), F32))
        out_specs.append(pl.BlockSpec((None, tm, GMLP_WIDTH), lambda b, t: (b, t, 0)))
    return pl.pallas_call(
        functools.partial(_mixer_kernel, tm=tm, pos0=pos0, emit_vn=emit_vn),
        grid=(B, T // tm),
        in_specs=[tok, per_b(POOL_HIST_PAD, POOL_WIDTH), per_b(CONV_HIST_PAD, CONV_WIDTH)]
                 + [_const_spec(a.shape) for a in weights],
        out_specs=out_specs,
        out_shape=out_shape,
        scratch_shapes=[pltpu.VMEM((POOL_HIST_PAD + tm, POOL_WIDTH), F32),
                        pltpu.VMEM((CONV_HIST_PAD + tm, CONV_WIDTH), F32)],
        compiler_params=pltpu.CompilerParams(dimension_semantics=("arbitrary", "arbitrary"),
                                             vmem_limit_bytes=VMEM_LIMIT_BYTES),
        name="mixer",
    )(x, pool_hist, conv_hist, *weights)


def _kv_kernel(m_ref, g_ref, wk_ref, wv_ref, k_ref, v_ref):
    mn = _rmsnorm(m_ref[...], g_ref[...]).astype(BF16)
    k_ref[...] = _dot(mn, wk_ref[...])
    v_ref[...] = _dot(mn, wv_ref[...])


def _memory_kv(mem, w, *, tm):
    B, M, D = mem.shape
    rows = mem.reshape(B * M, D)
    blk = pl.BlockSpec((tm, D), lambda i: (i, 0))
    k, v = pl.pallas_call(
        _kv_kernel,
        grid=(B * M // tm,),
        in_specs=[blk, _const_spec((1, D)), _const_spec((D, D)), _const_spec((D, D))],
        out_specs=[blk, blk],
        out_shape=[jax.ShapeDtypeStruct((B * M, D), F32)] * 2,
        compiler_params=pltpu.CompilerParams(dimension_semantics=("arbitrary",),
                                             vmem_limit_bytes=VMEM_LIMIT_BYTES),
        name="memory_kv",
    )(rows, w['g_mem'], w['w_xk'], w['w_xv'])
    return k.reshape(B, M, D), v.reshape(B, M, D)


def _route_select(xn, wr_hi_ref, wr_lo_ref, br_ref):
    x_hi = xn.astype(BF16)
    x_lo = (xn - x_hi.astype(F32)).astype(BF16)
    logits = (_dot(x_hi, wr_hi_ref[...]) + _dot(x_lo, wr_hi_ref[...]) + _dot(x_hi, wr_lo_ref[...])
              + br_ref[...])
    lane = lax.broadcasted_iota(jnp.int32, logits.shape, 1)
    neg = jnp.float32(-jnp.inf)
    big = jnp.int32(ROUTER_LANES)

    def first_argmax(vals):
        m = jnp.max(vals, axis=-1, keepdims=True)
        return m, jnp.min(jnp.where(vals == m, lane, big), axis=-1, keepdims=True)

    gl = jnp.where(lane < N_GROUPS, logits, neg)
    gmax, gsel = first_argmax(gl)
    pg = 1.0 / jnp.sum(jnp.exp(gl - gmax), axis=-1, keepdims=True)
    e_lo = LANE_E0 + gsel * EXP_PER_GROUP
    el = jnp.where((lane >= e_lo) & (lane < e_lo + EXP_PER_GROUP), logits, neg)
    v1, i1 = first_argmax(el)
    v2, i2 = first_argmax(jnp.where(lane == i1, neg, el))
    return lane, gsel, pg, i1, i2, v1, v2


def _route(xn, wr_hi_ref, wr_lo_ref, br_ref):
    lane, _, pg, i1, i2, v1, v2 = _route_select(xn, wr_hi_ref, wr_lo_ref, br_ref)
    r = jnp.exp(v2 - v1)
    pe1 = pg / (1.0 + r)
    pe2 = pg * r / (1.0 + r)
    return jnp.where(lane == i1, pe1, 0.0) + jnp.where(lane == i2, pe2, 0.0)


def _xattn_kernel(x_ref, k_ref, v_ref, g_ref, wq_ref, wo_ref, *rest, route):
    if route:
        gf_ref, wrh_ref, wrl_ref, br_ref, out_ref, meta_ref, cnt_ref, kb, vb, tri, run = rest
    else:
        out_ref, kb, vb = rest
    tm = x_ref.shape[0]

    @pl.when(pl.program_id(1) == 0)
    def _():
        kb[...] = k_ref[...].astype(BF16)
        vb[...] = v_ref[...].astype(BF16)

    x = x_ref[...]
    xn = _rmsnorm(x, g_ref[...]).astype(BF16)
    q = (_dot(xn, wq_ref[...]) * (XA_HDIM ** -0.5)).astype(BF16)
    heads = []
    for h in range(XA_HEADS):
        sl = slice(h * XA_HDIM, (h + 1) * XA_HDIM)
        s = lax.dot_general(q[:, sl], kb[:, sl], (((1,), (1,)), ((), ())), preferred_element_type=F32)
        e = jnp.exp(s - jnp.max(s, axis=-1, keepdims=True))
        pr = (e / jnp.sum(e, axis=-1, keepdims=True)).astype(BF16)
        heads.append(_dot(pr, vb[:, sl]).astype(BF16))
    o = jnp.concatenate(heads, axis=1)
    y = x + _dot(o, wo_ref[...])
    out_ref[...] = y
    if not route:
        return

    @pl.when((pl.program_id(0) == 0) & (pl.program_id(1) == 0))
    def _():
        r_i = lax.broadcasted_iota(jnp.int32, (tm, tm), 0)
        c_i = lax.broadcasted_iota(jnp.int32, (tm, tm), 1)
        tri[...] = jnp.where(r_i >= c_i, 1.0, 0.0).astype(BF16)
        run[...] = jnp.zeros_like(run)

    lane, gsel, _, i1, i2, _, _ = _route_select(_rmsnorm(y, gf_ref[...]), wrh_ref, wrl_ref, br_ref)
    e_lo = LANE_E0 + gsel * EXP_PER_GROUP
    a = jnp.minimum(i1, i2) - e_lo
    b = jnp.maximum(i1, i2) - e_lo
    pair = jnp.where(a == 0, 0, jnp.where(a == 1, 3, 5)) + (b - a - 1)
    bucket = gsel * PAIRS_PER_GROUP + pair
    onehot = jnp.where(lane == bucket, 1.0, 0.0)
    prefix = _dot(tri[...], onehot.astype(BF16))
    rank = jnp.sum(onehot * (prefix - 1.0 + run[...]), axis=-1, keepdims=True)
    run[...] = run[...] + prefix[tm - 1:tm, :]
    cnt_ref[...] = run[...]
    meta_ref[...] = jnp.where(lane == 0, bucket.astype(F32), jnp.where(lane == 1, rank, 0.0))


def _cross_attn(x, k, v, w, *, tm, route):
    B, T, D = x.shape
    tok = pl.BlockSpec((None, tm, D), lambda b, t: (b, t, 0))
    mem = pl.BlockSpec((None, MEM_LEN, D), lambda b, t: (b, 0, 0))
    in_specs = [tok, mem, mem, _const_spec((1, D)), _const_spec((D, D)), _const_spec((D, D))]
    args = [x, k, v, w['g_xa'], w['w_xq'], w['w_xo']]
    out_specs = [tok]
    out_shape = [jax.ShapeDtypeStruct((B, T, D), F32)]
    scratch = [pltpu.VMEM((MEM_LEN, D), BF16), pltpu.VMEM((MEM_LEN, D), BF16)]
    if route:
        in_specs += [_const_spec((1, D)), _const_spec((D, ROUTER_LANES)), _const_spec((D, ROUTER_LANES)),
                     _const_spec((1, ROUTER_LANES))]
        args += [w['g_ffn'], w['wr_hi'], w['wr_lo'], w['b_r']]
        out_specs += [pl.BlockSpec((None, tm, ROUTER_LANES), lambda b, t: (b, t, 0)),
                      pl.BlockSpec((1, ROUTER_LANES), lambda b, t: (0, 0))]
        out_shape += [jax.ShapeDtypeStruct((B, T, ROUTER_LANES), F32),
                      jax.ShapeDtypeStruct((1, ROUTER_LANES), F32)]
        scratch += [pltpu.VMEM((tm, tm), BF16), pltpu.VMEM((1, ROUTER_LANES), F32)]
    res = pl.pallas_call(
        functools.partial(_xattn_kernel, route=route),
        grid=(B, T // tm),
        in_specs=in_specs,
        out_specs=out_specs,
        out_shape=out_shape,
        scratch_shapes=scratch,
        compiler_params=pltpu.CompilerParams(dimension_semantics=("arbitrary", "arbitrary"),
                                             vmem_limit_bytes=VMEM_LIMIT_BYTES),
        name="cross_attn_route" if route else "cross_attn",
    )(*args)
    return res if route else res[0]


def _moe_dense_kernel(x_ref, g_ref, wrh_ref, wrl_ref, br_ref, wg_ref, wu_ref, wd_ref, gf_ref,
                      out_ref, xnb, comb, acc, *, final_norm):
    e = pl.program_id(1)

    @pl.when(e == 0)
    def _():
        xn = _rmsnorm(x_ref[...], g_ref[...])
        xnb[...] = xn.astype(BF16)
        comb[...] = _route(xn, wrh_ref, wrl_ref, br_ref)
        acc[...] = jnp.zeros_like(acc)

    lane = lax.broadcasted_iota(jnp.int32, comb.shape, 1)
    c = jnp.sum(jnp.where(lane == e + LANE_E0, comb[...], 0.0), axis=-1, keepdims=True)
    xb = xnb[...]
    hg = _dot(xb, wg_ref[...])
    he = hg * _sigmoid(hg) * _dot(xb, wu_ref[...])
    acc[...] += c * _dot(he.astype(BF16), wd_ref[...])

    @pl.when(e == N_EXPERTS - 1)
    def _():
        y = x_ref[...] + acc[...]
        out_ref[...] = _rmsnorm(y, gf_ref[...]) if final_norm else y


def _moe_dense(x, w, g_final, *, tm, final_norm):
    B, T, D = x.shape
    rows = x.reshape(B * T, D)
    tok = pl.BlockSpec((tm, D), lambda i, e: (i, 0))
    const = lambda shape: pl.BlockSpec(shape, lambda i, e: (0,) * len(shape), pipeline_mode=pl.Buffered(1))
    out = pl.pallas_call(
        functools.partial(_moe_dense_kernel, final_norm=final_norm),
        grid=(B * T // tm, N_EXPERTS),
        in_specs=[tok, const((1, D)), const((D, ROUTER_LANES)), const((D, ROUTER_LANES)),
                  const((1, ROUTER_LANES)),
                  pl.BlockSpec((None, D, D_EXPERT), lambda i, e: (e, 0, 0)),
                  pl.BlockSpec((None, D, D_EXPERT), lambda i, e: (e, 0, 0)),
                  pl.BlockSpec((None, D_EXPERT, D), lambda i, e: (e, 0, 0)),
                  const((1, D))],
        out_specs=tok,
        out_shape=jax.ShapeDtypeStruct((B * T, D), F32),
        scratch_shapes=[pltpu.VMEM((tm, D), BF16), pltpu.VMEM((tm, ROUTER_LANES), F32),
                        pltpu.VMEM((tm, D), F32)],
        compiler_params=pltpu.CompilerParams(dimension_semantics=("arbitrary", "arbitrary"),
                                             vmem_limit_bytes=VMEM_LIMIT_BYTES),
        name="moe_dense",
    )(rows, w['g_ffn'], w['wr_hi'], w['wr_lo'], w['b_r'], w['w_eg'], w['w_eu'], w['w_ed'], g_final)
    return out.reshape(B, T, D)


def _permute_kernel(idx_ref, x_hbm, o_hbm, sem, *, rows, scatter):
    i = pl.program_id(0)
    slot = i % 2

    def wait_step(s):
        pltpu.make_async_copy(x_hbm.at[pl.ds(0, rows)], o_hbm.at[pl.ds(0, rows)], sem.at[s]).wait()

    def issue(k, carry):
        r = i * rows + k
        j = idx_ref[0, k]
        src, dst = (r, j) if scatter else (j, r)
        pltpu.make_async_copy(x_hbm.at[pl.ds(src, 1)], o_hbm.at[pl.ds(dst, 1)], sem.at[slot]).start()
        return carry

    lax.fori_loop(0, rows, issue, 0, unroll=8)

    @pl.when(i > 0)
    def _():
        wait_step(1 - slot)

    @pl.when(i == pl.num_programs(0) - 1)
    def _():
        wait_step(slot)


def _permute_rows(x, idx, *, scatter):
    n, d = x.shape
    rows = PERMUTE_ROWS_PER_STEP
    assert n % rows == 0
    return pl.pallas_call(
        functools.partial(_permute_kernel, rows=rows, scatter=scatter),
        grid=(n // rows,),
        in_specs=[pl.BlockSpec((None, 1, rows), lambda i: (i, 0, 0), memory_space=pltpu.SMEM),
                  pl.BlockSpec(memory_space=pl.ANY)],
        out_specs=pl.BlockSpec(memory_space=pl.ANY),
        out_shape=jax.ShapeDtypeStruct((n, d), x.dtype),
        scratch_shapes=[pltpu.SemaphoreType.DMA((2,))],
        compiler_params=pltpu.CompilerParams(dimension_semantics=("arbitrary",), has_side_effects=True),
        name="permute_scatter" if scatter else "permute_gather",
    )(idx.reshape(n // rows, 1, rows), x)


def _moe_sorted_kernel(tile_ref, ea_ref, eb_ref, grp_ref, lo_ref, hi_ref, flag_ref,
                       x_ref, g_ref, wrh_ref, br_ref, wga_ref, wua_ref, wda_ref, wgb_ref, wub_ref, wdb_ref,
                       gf_ref, out_ref, *, final_norm):
    v = pl.program_id(0)
    lo, hi, flags = lo_ref[v], hi_ref[v], flag_ref[v]

    @pl.when((flags & 1) == 1)
    def _():
        out_ref[...] = jnp.zeros_like(out_ref)

    @pl.when(hi > lo)
    def _():
        xb = _rmsnorm(x_ref[...], g_ref[...]).astype(BF16)
        logits = _dot(xb, wrh_ref[...]) + br_ref[...]
        lane = lax.broadcasted_iota(jnp.int32, logits.shape, 1)
        pick = lambda l: jnp.sum(jnp.where(lane == l, logits, 0.0), axis=-1, keepdims=True)
        gl = jnp.where(lane < N_GROUPS, logits, jnp.float32(-jnp.inf))
        gmax = jnp.max(gl, axis=-1, keepdims=True)
        pg = jnp.exp(pick(grp_ref[v]) - gmax) / jnp.sum(jnp.exp(gl - gmax), axis=-1, keepdims=True)
        va, vb = pick(LANE_E0 + ea_ref[v]), pick(LANE_E0 + eb_ref[v])
        row = lax.broadcasted_iota(jnp.int32, (x_ref.shape[0], 1), 0)
        inside = (row >= lo) & (row < hi)
        acc = out_ref[...]
        for wt, wg_ref, wu_ref, wd_ref in ((pg / (1.0 + jnp.exp(vb - va)), wga_ref, wua_ref, wda_ref),
                                           (pg / (1.0 + jnp.exp(va - vb)), wgb_ref, wub_ref, wdb_ref)):
            hg = _dot(xb, wg_ref[...])
            he = jnp.where(inside, wt * (hg * _sigmoid(hg) * _dot(xb, wu_ref[...])), 0.0)
            acc = acc + _dot(he.astype(BF16), wd_ref[...])
        out_ref[...] = acc

    @pl.when((flags & 2) == 2)
    def _():
        y = x_ref[...] + out_ref[...]
        out_ref[...] = _rmsnorm(y, gf_ref[...]) if final_norm else y


def _moe_plan(meta, counts, n, tmg):
    i32 = jnp.int32
    bucket = meta[:, 0].astype(i32)
    rank = meta[:, 1].astype(i32)
    cnt = counts[0, :N_BUCKETS].astype(i32)
    ends = jnp.cumsum(cnt)
    offs = ends - cnt
    pos = jnp.take(offs, bucket) + rank
    nt = n // tmg
    cuts = jnp.sort(jnp.concatenate([jnp.arange(nt, dtype=i32) * tmg, offs[1:]]))
    nxt = jnp.concatenate([cuts[1:], jnp.full((1,), n, i32)])
    tile = jnp.minimum(cuts // tmg, nt - 1)
    bkt = jnp.minimum(jnp.sum((ends[None, :] <= cuts[:, None]).astype(i32), axis=1), N_BUCKETS - 1)
    change = (tile[1:] != tile[:-1]).astype(i32)
    one = jnp.ones((1,), i32)
    flags = jnp.concatenate([one, change]) + 2 * jnp.concatenate([change, one])
    grp = bkt // PAIRS_PER_GROUP
    pair = bkt % PAIRS_PER_GROUP
    ea = grp * EXP_PER_GROUP + jnp.take(jnp.array([0, 0, 0, 1, 1, 2], i32), pair)
    eb = grp * EXP_PER_GROUP + jnp.take(jnp.array([1, 2, 3, 2, 3, 3], i32), pair)
    return pos, (tile, ea, eb, grp, cuts - tile * tmg, nxt - tile * tmg, flags)


def _moe_sorted(x, meta, counts, w, g_final, *, tmg, final_norm):
    B, T, D = x.shape
    n = B * T
    pos, tables = _moe_plan(meta.reshape(n, ROUTER_LANES), counts, n, tmg)
    xs = _permute_rows(x.reshape(n, D), pos, scatter=True)
    tok = pl.BlockSpec((tmg, D), lambda v, tile, *_: (tile[v], 0))
    const = lambda shape: pl.BlockSpec(shape, lambda v, *_: (0,) * len(shape), pipeline_mode=pl.Buffered(1))
    w_a = lambda r, c: pl.BlockSpec((None, r, c), lambda v, tile, ea, eb, *_: (ea[v], 0, 0))
    w_b = lambda r, c: pl.BlockSpec((None, r, c), lambda v, tile, ea, eb, *_: (eb[v], 0, 0))
    ys = pl.pallas_call(
        functools.partial(_moe_sorted_kernel, final_norm=final_norm),
        grid_spec=pltpu.PrefetchScalarGridSpec(
            num_scalar_prefetch=len(tables),
            grid=(tables[0].shape[0],),
            in_specs=[tok, const((1, D)), const((D, ROUTER_LANES)), const((1, ROUTER_LANES)),
                      w_a(D, D_EXPERT), w_a(D, D_EXPERT), w_a(D_EXPERT, D),
                      w_b(D, D_EXPERT), w_b(D, D_EXPERT), w_b(D_EXPERT, D), const((1, D))],
            out_specs=tok),
        out_shape=jax.ShapeDtypeStruct((n, D), F32),
        compiler_params=pltpu.CompilerParams(dimension_semantics=("arbitrary",),
                                             vmem_limit_bytes=VMEM_LIMIT_BYTES),
        name="moe_sorted",
    )(*tables, xs, w['g_ffn'], w['wr_hi'], w['b_r'], w['w_eg'], w['w_eu'], w['w_ed'],
      w['w_eg'], w['w_eu'], w['w_ed'], g_final)
    return _permute_rows(ys, pos, scatter=False).reshape(B, T, D)


def _layer_weights(l, g_mix, w_in, b_gate, w_pool, s_pool, g_gv, b_gv, w_s, b_s, w_gmlp_out,
                   w_dw, b_dw, g_cln, b_cln, w_conv_out, w_out, g_xa, g_mem, w_xq, w_xk, w_xv, w_xo,
                   g_ffn, w_rg, b_rg, w_re, b_re, w_eg, w_eu, w_ed):
    row = lambda a: a[l].reshape(1, -1)
    wp = jnp.zeros((POOL_WIDTH, D_MODEL), F32)
    gout = D_MODEL // len(POOL_WINDOWS)
    for g in range(len(POOL_WINDOWS)):
        wp = wp.at[g * POOL_GDIM:(g + 1) * POOL_GDIM, g * gout:(g + 1) * gout].set(w_pool[l, g])
    wr = jnp.zeros((D_MODEL, ROUTER_LANES), F32)
    wr = wr.at[:, :N_GROUPS].set(w_rg[l]).at[:, LANE_E0:LANE_E0 + N_EXPERTS].set(w_re[l])
    wr_hi = wr.astype(BF16)
    br = jnp.zeros((1, ROUTER_LANES), F32)
    br = br.at[0, :N_GROUPS].set(b_rg[l]).at[0, LANE_E0:LANE_E0 + N_EXPERTS].set(b_re[l])
    return {
        'g_mix': row(g_mix), 'w_in': w_in[l].astype(BF16), 'b_gate': b_gate[l],
        'w_pool_bd': wp.astype(BF16), 's_pool': row(s_pool), 'g_gv': row(g_gv), 'b_gv': row(b_gv),
        'w_s': w_s[l], 'b_s_t': jnp.transpose(b_s[l]), 'w_gmlp_out': w_gmlp_out[l].astype(BF16),
        'w_dw': w_dw[l], 'b_dw': row(b_dw), 'g_cln': row(g_cln), 'b_cln': row(b_cln),
        'w_conv_out': w_conv_out[l].astype(BF16), 'w_out': w_out[l].astype(BF16),
        'g_xa': row(g_xa), 'g_mem': row(g_mem), 'w_xq': w_xq[l].astype(BF16),
        'w_xk': w_xk[l].astype(BF16), 'w_xv': w_xv[l].astype(BF16), 'w_xo': w_xo[l].astype(BF16),
        'g_ffn': row(g_ffn), 'wr_hi': wr_hi, 'wr_lo': (wr - wr_hi.astype(F32)).astype(BF16), 'b_r': br,
        'w_eg': w_eg[l].astype(BF16), 'w_eu': w_eu[l].astype(BF16), 'w_ed': w_ed[l].astype(BF16),
    }


def _moe(x, k, v, w, g_final, *, tm, final_norm):
    n = x.shape[0] * x.shape[1]
    if n >= MOE_SORTED_MIN_ROWS:
        x, meta, counts = _cross_attn(x, k, v, w, tm=tm, route=True)
        return _moe_sorted(x, meta, counts, w, g_final, tmg=MOE_SORTED_TILE, final_norm=final_norm)
    x = _cross_attn(x, k, v, w, tm=tm, route=False)
    return _moe_dense(x, w, g_final, tm=_token_tile(n), final_norm=final_norm)


def _pad_hist(h, rows):
    return jnp.pad(h, ((0, 0), (rows - h.shape[1], 0), (0, 0)))


def _token_tile(t):
    return min(512, t)


def kernel(x_prompt, x_sample, mem_prompt, cache_pool, cache_conv, cache_mem_k, cache_mem_v, g_mix, w_in, b_gate, w_pool, s_pool, g_gv, b_gv, w_s, b_s, w_gmlp_out, w_dw, b_dw, g_cln, b_cln, w_conv_out, w_out, g_xa, g_mem, w_xq, w_xk, w_xv, w_xo, g_ffn, w_rg, b_rg, w_re, b_re, w_eg, w_eu, w_ed, g_final):
    depth = w_in.shape[0]
    bp, tp, _ = x_prompt.shape
    bs, ts, _ = x_sample.shape
    tmp, tms = _token_tile(tp), _token_tile(ts)
    gf = g_final.reshape(1, -1)
    xp, xs = x_prompt, x_sample
    pool0 = jnp.zeros((bp, POOL_HIST_PAD, POOL_WIDTH), F32)
    conv0 = jnp.zeros((bp, CONV_HIST_PAD, CONV_WIDTH), F32)
    outs = {k: [] for k in ('pp', 'pc', 'pk', 'pv', 'sp', 'sc', 'sv')}
    for l in range(depth):
        w = _layer_weights(l, g_mix, w_in, b_gate, w_pool, s_pool, g_gv, b_gv, w_s, b_s, w_gmlp_out,
                           w_dw, b_dw, g_cln, b_cln, w_conv_out, w_out, g_xa, g_mem, w_xq, w_xk, w_xv,
                           w_xo, g_ffn, w_rg, b_rg, w_re, b_re, w_eg, w_eu, w_ed)
        last = l == depth - 1
        xp, ph, ch = _mixer(xp, pool0, conv0, w, tm=tmp, pos0=0, emit_vn=False)
        mk, mv = _memory_kv(mem_prompt, w, tm=_token_tile(bp * MEM_LEN))
        xp = _moe(xp, mk, mv, w, gf, tm=tmp, final_norm=last)
        outs['pp'].append(ph[:, POOL_HIST_PAD - POOL_HIST:])
        outs['pc'].append(ch[:, CONV_HIST_PAD - CONV_HIST:])
        outs['pk'].append(mk.reshape(bp, MEM_LEN, XA_HEADS, XA_HDIM))
        outs['pv'].append(mv.reshape(bp, MEM_LEN, XA_HEADS, XA_HDIM))
        xs, sh, sc, vn = _mixer(xs, _pad_hist(cache_pool[l], POOL_HIST_PAD),
                                _pad_hist(cache_conv[l], CONV_HIST_PAD), w,
                                tm=tms, pos0=PAST_LEN, emit_vn=True)
        xs = _moe(xs, cache_mem_k[l].reshape(bs, MEM_LEN, D_MODEL),
                  cache_mem_v[l].reshape(bs, MEM_LEN, D_MODEL), w, gf, tm=tms, final_norm=last)
        outs['sp'].append(sh[:, POOL_HIST_PAD - POOL_HIST:])
        outs['sc'].append(sc[:, CONV_HIST_PAD - CONV_HIST:])
        outs['sv'].append(vn)
    st = lambda k: jnp.stack(outs[k], axis=0)
    return (xp, xs, st('pp'), st('pc'), st('pk'), st('pv'), st('sp'), st('sc'), st('sv'))
```

```python
import functools

import jax
import jax.numpy as jnp
from jax import lax
from jax.experimental import pallas as pl
from jax.experimental.pallas import tpu as pltpu

F32 = jnp.float32
BF16 = jnp.bfloat16

EPS = 1e-6
D_MODEL = 1024
PAST_LEN = 4096
CHUNK = 64
POOL_WIDTH = 256
POOL_WINDOWS = (2, 4, 8, 16)
POOL_GDIM = 64
POOL_HIST = 15
POOL_HIST_PAD = 16
GMLP_WIDTH = 512
GMLP_HEADS = 4
GMLP_HDIM = 128
GMLP_CHUNK = 128
CONV_WIDTH = 256
CONV_K = 31
CONV_HIST = 30
CONV_HIST_PAD = 32
SUBLANES = 8
N_BRANCH = 3
O_POOL = 0
O_GMLP = POOL_WIDTH
O_CONV = O_GMLP + 2 * GMLP_WIDTH
O_GATE = O_CONV + 2 * CONV_WIDTH
IN_COLS = O_GATE + N_BRANCH * D_MODEL
MEM_LEN = 256
XA_HEADS = 4
XA_HDIM = 256
N_GROUPS = 4
EXP_PER_GROUP = 4
N_EXPERTS = 16
D_EXPERT = 256
ROUTER_LANES = 128
LANE_E0 = N_GROUPS

VMEM_LIMIT_BYTES = 56 * 1024 * 1024


def _dot(a, b):
    return jnp.dot(a, b, preferred_element_type=F32)


def _rmsnorm(x, g):
    return x * lax.rsqrt(jnp.mean(x * x, axis=-1, keepdims=True) + EPS) * g


def _layernorm(x, g, b):
    mu = jnp.mean(x, axis=-1, keepdims=True)
    xc = x - mu
    var = jnp.mean(xc * xc, axis=-1, keepdims=True)
    return xc * lax.rsqrt(var + EPS) * g + b


def _sigmoid(x):
    return 1.0 / (1.0 + jnp.exp(-x))


def _const_spec(shape):
    nd = len(shape)
    return pl.BlockSpec(shape, lambda *_: (0,) * nd, pipeline_mode=pl.Buffered(1))


def _mixer_kernel(x_ref, ph_ref, ch_ref, g_ref, win_ref, bg_ref, wp_ref, sp_ref, ggv_ref, bgv_ref,
                  ws_ref, bs_ref, wgo_ref, wdw_ref, bdw_ref, gcl_ref, bcl_ref, wco_ref, wo_ref,
                  *rest, tm, pos0, emit_vn):
    if emit_vn:
        out_ref, pt_ref, at_ref, vn_ref, pbuf, abuf, sbuf = rest
    else:
        out_ref, pt_ref, at_ref, pbuf, abuf, sbuf = rest
        vn_ref = None
    t = pl.program_id(1)

    @pl.when(t == 0)
    def _():
        pbuf[0:POOL_HIST_PAD, :] = ph_ref[...]
        abuf[0:CONV_HIST_PAD, :] = ch_ref[...]

    x = x_ref[...]
    xn = _rmsnorm(x, g_ref[...]).astype(BF16)

    p = _dot(xn, win_ref[:, O_POOL:O_POOL + POOL_WIDTH])
    pbuf[POOL_HIST_PAD:POOL_HIST_PAD + tm, :] = p
    pos = (pos0 + t * tm + lax.broadcasted_iota(jnp.int32, (tm, 1), 0) + 1).astype(F32)
    lane = lax.broadcasted_iota(jnp.int32, (tm, 128), 1)

    def prow(k, c):
        return pbuf[POOL_HIST_PAD - k:POOL_HIST_PAD - k + tm, c * 128:(c + 1) * 128]

    means = []
    for c in range(2):
        w_lo, w_hi = POOL_WINDOWS[2 * c], POOL_WINDOWS[2 * c + 1]
        s_lo = prow(0, c)
        for k in range(1, w_lo):
            s_lo = s_lo + prow(k, c)
        s_hi = s_lo
        for k in range(w_lo, w_hi):
            s_hi = s_hi + prow(k, c)
        m_lo = s_lo / jnp.minimum(float(w_lo), pos)
        m_hi = s_hi / jnp.minimum(float(w_hi), pos)
        means.append(jnp.where(lane < POOL_GDIM, m_lo, m_hi))
    d = (jnp.concatenate(means, axis=1) - p).astype(BF16)
    ya = _dot(d, wp_ref[...]) * sp_ref[...]
    merged = _sigmoid(_dot(xn, win_ref[:, O_GATE:O_GATE + D_MODEL]) + bg_ref[0:1, :]) * ya

    z = jax.nn.gelu(_dot(xn, win_ref[:, O_GMLP:O_GMLP + 2 * GMLP_WIDTH]))
    u = z[:, :GMLP_WIDTH]
    vn = _layernorm(z[:, GMLP_WIDTH:], ggv_ref[...], bgv_ref[...])
    if emit_vn:
        vn_ref[...] = vn
    vb = vn.astype(BF16)
    cr = min(GMLP_CHUNK, tm)
    bi = lax.broadcasted_iota(jnp.int32, (cr, cr), 0) // CHUNK
    bj = lax.broadcasted_iota(jnp.int32, (cr, cr), 1) // CHUNK
    ws = [jnp.where(bi >= bj, ws_ref[h, 0:cr, 0:cr], 0.0).astype(BF16) for h in range(GMLP_HEADS)]
    rows = []
    for c0 in range(0, tm, cr):
        heads = []
        for h in range(GMLP_HEADS):
            vh = vb[c0:c0 + cr, h * GMLP_HDIM:(h + 1) * GMLP_HDIM]
            heads.append(_dot(ws[h], vh) + bs_ref[0:cr, h:h + 1])
        rows.append(jnp.concatenate(heads, axis=1))
    mixed = rows[0] if len(rows) == 1 else jnp.concatenate(rows, axis=0)
    yb = _dot((u * mixed).astype(BF16), wgo_ref[...])
    merged = merged + _sigmoid(_dot(xn, win_ref[:, O_GATE + D_MODEL:O_GATE + 2 * D_MODEL]) + bg_ref[1:2, :]) * yb

    cc = _dot(xn, win_ref[:, O_CONV:O_CONV + 2 * CONV_WIDTH])
    a = cc[:, :CONV_WIDTH] * _sigmoid(cc[:, CONV_WIDTH:])
    abuf[CONV_HIST_PAD:CONV_HIST_PAD + tm, :] = a
    span = tm + CONV_HIST_PAD - SUBLANES
    for r in range(1, SUBLANES):
        sbuf[r - 1, :, :] = abuf[r:r + span, :]

    def tap(k, r0, n):
        q, r = divmod(CONV_HIST_PAD - CONV_HIST + k, SUBLANES)
        if r == 0:
            return abuf[SUBLANES * q + r0:SUBLANES * q + r0 + n, :]
        return sbuf[r - 1, SUBLANES * q + r0:SUBLANES * q + r0 + n, :]

    rc = min(64, tm)
    hs = []
    for r0 in range(0, tm, rc):
        acc = tap(0, r0, rc) * wdw_ref[0:1, :]
        for k in range(1, CONV_K):
            acc = acc + tap(k, r0, rc) * wdw_ref[k:k + 1, :]
        hln = _layernorm(acc + bdw_ref[...], gcl_ref[...], bcl_ref[...])
        hs.append((hln * _sigmoid(hln)).astype(BF16))
    hc = hs[0] if len(hs) == 1 else jnp.concatenate(hs, axis=0)
    yc = _dot(hc, wco_ref[...])
    merged = merged + _sigmoid(_dot(xn, win_ref[:, O_GATE + 2 * D_MODEL:O_GATE + 3 * D_MODEL]) + bg_ref[2:3, :]) * yc

    out_ref[...] = x + _dot(merged.astype(BF16), wo_ref[...])

    p_tail = pbuf[tm:tm + POOL_HIST_PAD, :]
    a_tail = abuf[tm:tm + CONV_HIST_PAD, :]
    pbuf[0:POOL_HIST_PAD, :] = p_tail
    abuf[0:CONV_HIST_PAD, :] = a_tail
    pt_ref[...] = p_tail
    at_ref[...] = a_tail


def _mixer(x, pool_hist, conv_hist, w, *, tm, pos0, emit_vn):
    B, T, D = x.shape
    assert T % tm == 0 and tm >= CONV_HIST_PAD and (tm % GMLP_CHUNK == 0 or tm == T)
    tok = pl.BlockSpec((None, tm, D), lambda b, t: (b, t, 0))
    per_b = lambda r, c: pl.BlockSpec((None, r, c), lambda b, t: (b, 0, 0))
    weights = [w['g_mix'], w['w_in'], w['b_gate'], w['w_pool_bd'], w['s_pool'], w['g_gv'], w['b_gv'],
               w['w_s'], w['b_s_t'], w['w_gmlp_out'], w['w_dw'], w['b_dw'], w['g_cln'], w['b_cln'],
               w['w_conv_out'], w['w_out']]
    out_shape = [jax.ShapeDtypeStruct((B, T, D), F32),
                 jax.ShapeDtypeStruct((B, POOL_HIST_PAD, POOL_WIDTH), F32),
                 jax.ShapeDtypeStruct((B, CONV_HIST_PAD, CONV_WIDTH), F32)]
    out_specs = [tok, per_b(POOL_HIST_PAD, POOL_WIDTH), per_b(CONV_HIST_PAD, CONV_WIDTH)]
    if emit_vn:
        out_shape.append(jax.ShapeDtypeStruct((B, T, GMLP_WIDTH), F32))
        out_specs.append(pl.BlockSpec((None, tm, GMLP_WIDTH), lambda b, t: (b, t, 0)))
    return pl.pallas_call(
        functools.partial(_mixer_kernel, tm=tm, pos0=pos0, emit_vn=emit_vn),
        grid=(B, T // tm),
        in_specs=[tok, per_b(POOL_HIST_PAD, POOL_WIDTH), per_b(CONV_HIST_PAD, CONV_WIDTH)]
                 + [_const_spec(a.shape) for a in weights],
        out_specs=out_specs,
        out_shape=out_shape,
        scratch_shapes=[pltpu.VMEM((POOL_HIST_PAD + tm, POOL_WIDTH), F32),
                        pltpu.VMEM((CONV_HIST_PAD + tm, CONV_WIDTH), F32),
                        pltpu.VMEM((SUBLANES - 1, CONV_HIST_PAD + tm - SUBLANES, CONV_WIDTH), F32)],
        compiler_params=pltpu.CompilerParams(dimension_semantics=("arbitrary", "arbitrary"),
                                             vmem_limit_bytes=VMEM_LIMIT_BYTES),
        name="mixer",
    )(x, pool_hist, conv_hist, *weights)


def _kv_kernel(m_ref, g_ref, wk_ref, wv_ref, k_ref, v_ref):
    mn = _rmsnorm(m_ref[...], g_ref[...]).astype(BF16)
    k_ref[...] = _dot(mn, wk_ref[...])
    v_ref[...] = _dot(mn, wv_ref[...])


def _memory_kv(mem, w, *, tm):
    B, M, D = mem.shape
    rows = mem.reshape(B * M, D)
    blk = pl.BlockSpec((tm, D), lambda i: (i, 0))
    k, v = pl.pallas_call(
        _kv_kernel,
        grid=(B * M // tm,),
        in_specs=[blk, _const_spec((1, D)), _const_spec((D, D)), _const_spec((D, D))],
        out_specs=[blk, blk],
        out_shape=[jax.ShapeDtypeStruct((B * M, D), F32)] * 2,
        compiler_params=pltpu.CompilerParams(dimension_semantics=("arbitrary",),
                                             vmem_limit_bytes=VMEM_LIMIT_BYTES),
        name="memory_kv",
    )(rows, w['g_mem'], w['w_xk'], w['w_xv'])
    return k.reshape(B, M, D), v.reshape(B, M, D)


def _xattn_kernel(x_ref, k_ref, v_ref, g_ref, wq_ref, wo_ref, out_ref, kb, vb):
    @pl.when(pl.program_id(1) == 0)
    def _():
        kb[...] = k_ref[...].astype(BF16)
        vb[...] = v_ref[...].astype(BF16)

    x = x_ref[...]
    xn = _rmsnorm(x, g_ref[...]).astype(BF16)
    q = (_dot(xn, wq_ref[...]) * (XA_HDIM ** -0.5)).astype(BF16)
    heads = []
    for h in range(XA_HEADS):
        sl = slice(h * XA_HDIM, (h + 1) * XA_HDIM)
        s = lax.dot_general(q[:, sl], kb[:, sl], (((1,), (1,)), ((), ())), preferred_element_type=F32)
        e = jnp.exp(s - jnp.max(s, axis=-1, keepdims=True))
        pr = (e / jnp.sum(e, axis=-1, keepdims=True)).astype(BF16)
        heads.append(_dot(pr, vb[:, sl]).astype(BF16))
    o = jnp.concatenate(heads, axis=1)
    out_ref[...] = x + _dot(o, wo_ref[...])


def _cross_attn(x, k, v, w, *, tm):
    B, T, D = x.shape
    tok = pl.BlockSpec((None, tm, D), lambda b, t: (b, t, 0))
    mem = pl.BlockSpec((None, MEM_LEN, D), lambda b, t: (b, 0, 0))
    return pl.pallas_call(
        _xattn_kernel,
        grid=(B, T // tm),
        in_specs=[tok, mem, mem, _const_spec((1, D)), _const_spec((D, D)), _const_spec((D, D))],
        out_specs=tok,
        out_shape=jax.ShapeDtypeStruct((B, T, D), F32),
        scratch_shapes=[pltpu.VMEM((MEM_LEN, D), BF16), pltpu.VMEM((MEM_LEN, D), BF16)],
        compiler_params=pltpu.CompilerParams(dimension_semantics=("arbitrary", "arbitrary"),
                                             vmem_limit_bytes=VMEM_LIMIT_BYTES),
        name="cross_attn",
    )(x, k, v, w['g_xa'], w['w_xq'], w['w_xo'])


def _route(xn, wr_hi_ref, wr_lo_ref, br_ref):
    x_hi = xn.astype(BF16)
    x_lo = (xn - x_hi.astype(F32)).astype(BF16)
    logits = (_dot(x_hi, wr_hi_ref[...]) + _dot(x_lo, wr_hi_ref[...]) + _dot(x_hi, wr_lo_ref[...])
              + br_ref[...])
    lane = lax.broadcasted_iota(jnp.int32, logits.shape, 1)
    neg = jnp.float32(-jnp.inf)
    big = jnp.int32(ROUTER_LANES)

    def first_argmax(vals):
        m = jnp.max(vals, axis=-1, keepdims=True)
        return m, jnp.min(jnp.where(vals == m, lane, big), axis=-1, keepdims=True)

    gl = jnp.where(lane < N_GROUPS, logits, neg)
    gmax, gsel = first_argmax(gl)
    pg = 1.0 / jnp.sum(jnp.exp(gl - gmax), axis=-1, keepdims=True)
    e_lo = LANE_E0 + gsel * EXP_PER_GROUP
    el = jnp.where((lane >= e_lo) & (lane < e_lo + EXP_PER_GROUP), logits, neg)
    v1, i1 = first_argmax(el)
    v2, i2 = first_argmax(jnp.where(lane == i1, neg, el))
    r = jnp.exp(v2 - v1)
    pe1 = pg / (1.0 + r)
    pe2 = pg * r / (1.0 + r)
    return jnp.where(lane == i1, pe1, 0.0) + jnp.where(lane == i2, pe2, 0.0)


def _moe_dense_kernel(x_ref, g_ref, wrh_ref, wrl_ref, br_ref, wgu_ref, wd_ref, gf_ref, out_ref, he_buf,
                      *, final_norm):
    x = x_ref[...]
    xn = _rmsnorm(x, g_ref[...])
    comb = _route(xn, wrh_ref, wrl_ref, br_ref)
    xb = xn.astype(BF16)
    lane = lax.broadcasted_iota(jnp.int32, comb.shape, 1)
    for e in range(N_EXPERTS):
        c = jnp.sum(jnp.where(lane == e + LANE_E0, comb, 0.0), axis=-1, keepdims=True)
        h = _dot(xb, wgu_ref[e])
        hg = h[:, :D_EXPERT]
        he = c * (hg * _sigmoid(hg) * h[:, D_EXPERT:])
        he_buf[:, e * D_EXPERT:(e + 1) * D_EXPERT] = he.astype(BF16)
    y = x + _dot(he_buf[...], wd_ref[...])
    out_ref[...] = _rmsnorm(y, gf_ref[...]) if final_norm else y


def _moe_dense(x, w, g_final, *, tm, final_norm):
    B, T, D = x.shape
    rows = x.reshape(B * T, D)
    tok = pl.BlockSpec((tm, D), lambda i: (i, 0))
    out = pl.pallas_call(
        functools.partial(_moe_dense_kernel, final_norm=final_norm),
        grid=(B * T // tm,),
        in_specs=[tok, _const_spec((1, D)), _const_spec((D, ROUTER_LANES)), _const_spec((D, ROUTER_LANES)),
                  _const_spec((1, ROUTER_LANES)), _const_spec((N_EXPERTS, D, 2 * D_EXPERT)),
                  _const_spec((N_EXPERTS * D_EXPERT, D)), _const_spec((1, D))],
        out_specs=tok,
        out_shape=jax.ShapeDtypeStruct((B * T, D), F32),
        scratch_shapes=[pltpu.VMEM((tm, N_EXPERTS * D_EXPERT), BF16)],
        compiler_params=pltpu.CompilerParams(dimension_semantics=("arbitrary",),
                                             vmem_limit_bytes=VMEM_LIMIT_BYTES),
        name="moe_dense",
    )(rows, w['g_ffn'], w['wr_hi'], w['wr_lo'], w['b_r'], w['w_egu'], w['w_ed'], g_final)
    return out.reshape(B, T, D)


def _layer_weights(l, g_mix, w_in, b_gate, w_pool, s_pool, g_gv, b_gv, w_s, b_s, w_gmlp_out,
                   w_dw, b_dw, g_cln, b_cln, w_conv_out, w_out, g_xa, g_mem, w_xq, w_xk, w_xv, w_xo,
                   g_ffn, w_rg, b_rg, w_re, b_re, w_eg, w_eu, w_ed):
    row = lambda a: a[l].reshape(1, -1)
    wp = jnp.zeros((POOL_WIDTH, D_MODEL), F32)
    gout = D_MODEL // len(POOL_WINDOWS)
    for g in range(len(POOL_WINDOWS)):
        wp = wp.at[g * POOL_GDIM:(g + 1) * POOL_GDIM, g * gout:(g + 1) * gout].set(w_pool[l, g])
    wr = jnp.zeros((D_MODEL, ROUTER_LANES), F32)
    wr = wr.at[:, :N_GROUPS].set(w_rg[l]).at[:, LANE_E0:LANE_E0 + N_EXPERTS].set(w_re[l])
    wr_hi = wr.astype(BF16)
    br = jnp.zeros((1, ROUTER_LANES), F32)
    br = br.at[0, :N_GROUPS].set(b_rg[l]).at[0, LANE_E0:LANE_E0 + N_EXPERTS].set(b_re[l])
    return {
        'g_mix': row(g_mix), 'w_in': w_in[l].astype(BF16), 'b_gate': b_gate[l],
        'w_pool_bd': wp.astype(BF16), 's_pool': row(s_pool), 'g_gv': row(g_gv), 'b_gv': row(b_gv),
        'w_s': w_s[l], 'b_s_t': jnp.transpose(b_s[l]), 'w_gmlp_out': w_gmlp_out[l].astype(BF16),
        'w_dw': w_dw[l], 'b_dw': row(b_dw), 'g_cln': row(g_cln), 'b_cln': row(b_cln),
        'w_conv_out': w_conv_out[l].astype(BF16), 'w_out': w_out[l].astype(BF16),
        'g_xa': row(g_xa), 'g_mem': row(g_mem), 'w_xq': w_xq[l].astype(BF16),
        'w_xk': w_xk[l].astype(BF16), 'w_xv': w_xv[l].astype(BF16), 'w_xo': w_xo[l].astype(BF16),
        'g_ffn': row(g_ffn), 'wr_hi': wr_hi, 'wr_lo': (wr - wr_hi.astype(F32)).astype(BF16), 'b_r': br,
        'w_egu': jnp.concatenate([w_eg[l], w_eu[l]], axis=-1).astype(BF16),
        'w_ed': w_ed[l].reshape(N_EXPERTS * D_EXPERT, D_MODEL).astype(BF16),
    }


def _pad_hist(h, rows):
    return jnp.pad(h, ((0, 0), (rows - h.shape[1], 0), (0, 0)))


def _token_tile(t):
    return min(512, t)


def kernel(x_prompt, x_sample, mem_prompt, cache_pool, cache_conv, cache_mem_k, cache_mem_v, g_mix, w_in, b_gate, w_pool, s_pool, g_gv, b_gv, w_s, b_s, w_gmlp_out, w_dw, b_dw, g_cln, b_cln, w_conv_out, w_out, g_xa, g_mem, w_xq, w_xk, w_xv, w_xo, g_ffn, w_rg, b_rg, w_re, b_re, w_eg, w_eu, w_ed, g_final):
    depth = w_in.shape[0]
    bp, tp, _ = x_prompt.shape
    bs, ts, _ = x_sample.shape
    tmp, tms = _token_tile(tp), _token_tile(ts)
    gf = g_final.reshape(1, -1)
    xp, xs = x_prompt, x_sample
    pool0 = jnp.zeros((bp, POOL_HIST_PAD, POOL_WIDTH), F32)
    conv0 = jnp.zeros((bp, CONV_HIST_PAD, CONV_WIDTH), F32)
    outs = {k: [] for k in ('pp', 'pc', 'pk', 'pv', 'sp', 'sc', 'sv')}
    for l in range(depth):
        w = _layer_weights(l, g_mix, w_in, b_gate, w_pool, s_pool, g_gv, b_gv, w_s, b_s, w_gmlp_out,
                           w_dw, b_dw, g_cln, b_cln, w_conv_out, w_out, g_xa, g_mem, w_xq, w_xk, w_xv,
                           w_xo, g_ffn, w_rg, b_rg, w_re, b_re, w_eg, w_eu, w_ed)
        last = l == depth - 1
        xp, ph, ch = _mixer(xp, pool0, conv0, w, tm=tmp, pos0=0, emit_vn=False)
        mk, mv = _memory_kv(mem_prompt, w, tm=_token_tile(bp * MEM_LEN))
        xp = _cross_attn(xp, mk, mv, w, tm=tmp)
        xp = _moe_dense(xp, w, gf, tm=tmp, final_norm=last)
        outs['pp'].append(ph[:, POOL_HIST_PAD - POOL_HIST:])
        outs['pc'].append(ch[:, CONV_HIST_PAD - CONV_HIST:])
        outs['pk'].append(mk.reshape(bp, MEM_LEN, XA_HEADS, XA_HDIM))
        outs['pv'].append(mv.reshape(bp, MEM_LEN, XA_HEADS, XA_HDIM))
        xs, sh, sc, vn = _mixer(xs, _pad_hist(cache_pool[l], POOL_HIST_PAD),
                                _pad_hist(cache_conv[l], CONV_HIST_PAD), w,
                                tm=tms, pos0=PAST_LEN, emit_vn=True)
        xs = _cross_attn(xs, cache_mem_k[l].reshape(bs, MEM_LEN, D_MODEL),
                         cache_mem_v[l].reshape(bs, MEM_LEN, D_MODEL), w, tm=tms)
        xs = _moe_dense(xs, w, gf, tm=_token_tile(bs * ts), final_norm=last)
        outs['sp'].append(sh[:, POOL_HIST_PAD - POOL_HIST:])
        outs['sc'].append(sc[:, CONV_HIST_PAD - CONV_HIST:])
        outs['sv'].append(vn)
    st = lambda k: jnp.stack(outs[k], axis=0)
    return (xp, xs, st('pp'), st('pc'), st('pk'), st('pv'), st('sp'), st('sc'), st('sv'))
```

```python
import functools

import jax
import jax.numpy as jnp
from jax import lax
from jax.experimental import pallas as pl
from jax.experimental.pallas import tpu as pltpu

F32 = jnp.float32
BF16 = jnp.bfloat16

EPS = 1e-6
D_MODEL = 1024
PAST_LEN = 4096
CHUNK = 64
POOL_WIDTH = 256
POOL_WINDOWS = (2, 4, 8, 16)
POOL_GDIM = 64
POOL_HIST = 15
POOL_HIST_PAD = 16
GMLP_WIDTH = 512
GMLP_HEADS = 4
GMLP_HDIM = 128
GMLP_CHUNK = 128
CONV_WIDTH = 256
CONV_K = 31
CONV_HIST = 30
CONV_HIST_PAD = 32
SUBLANES = 8
MIXER_ROW_BLOCK = 512
CONV_ROWS = 64
N_BRANCH = 3
O_POOL = 0
O_GMLP = POOL_WIDTH
O_CONV = O_GMLP + 2 * GMLP_WIDTH
O_GATE = O_CONV + 2 * CONV_WIDTH
IN_COLS = O_GATE + N_BRANCH * D_MODEL
MEM_LEN = 256
XA_HEADS = 4
XA_HDIM = 256
N_GROUPS = 4
EXP_PER_GROUP = 4
N_EXPERTS = 16
D_EXPERT = 256
ROUTER_LANES = 128
LANE_E0 = N_GROUPS
PAIRS_PER_GROUP = 6
N_BUCKETS = N_GROUPS * PAIRS_PER_GROUP
MOE_SORTED_TILE = 256
MOE_SORTED_MIN_ROWS = 1024

VMEM_LIMIT_BYTES = 56 * 1024 * 1024


def _dot(a, b):
    return jnp.dot(a, b, preferred_element_type=F32)


def _rmsnorm(x, g):
    return x * lax.rsqrt(jnp.mean(x * x, axis=-1, keepdims=True) + EPS) * g


def _layernorm(x, g, b):
    mu = jnp.mean(x, axis=-1, keepdims=True)
    xc = x - mu
    var = jnp.mean(xc * xc, axis=-1, keepdims=True)
    return xc * lax.rsqrt(var + EPS) * g + b


def _sigmoid(x):
    return 1.0 / (1.0 + jnp.exp(-x))


def _const_spec(shape):
    nd = len(shape)
    return pl.BlockSpec(shape, lambda *_: (0,) * nd, pipeline_mode=pl.Buffered(1))


def _mixer_kernel(x_ref, ph_ref, ch_ref, g_ref, win_ref, bg_ref, wp_ref, sp_ref, ggv_ref, bgv_ref,
                  ws_ref, bs_ref, wgo_ref, wdw_ref, bdw_ref, gcl_ref, bcl_ref, wco_ref, wo_ref,
                  *rest, tm, pos0, emit_vn):
    if emit_vn:
        out_ref, pt_ref, at_ref, vn_ref, pbuf, abuf, sbuf = rest
    else:
        out_ref, pt_ref, at_ref, pbuf, abuf, sbuf = rest
        vn_ref = None
    t = pl.program_id(1)

    @pl.when(t == 0)
    def _():
        pbuf[0:POOL_HIST_PAD, :] = ph_ref[...]
        abuf[0:CONV_HIST_PAD, :] = ch_ref[...]

    rb = min(MIXER_ROW_BLOCK, tm)
    cr = min(GMLP_CHUNK, tm)
    bi = lax.broadcasted_iota(jnp.int32, (cr, cr), 0) // CHUNK
    bj = lax.broadcasted_iota(jnp.int32, (cr, cr), 1) // CHUNK
    ws = [jnp.where(bi >= bj, ws_ref[h, 0:cr, 0:cr], 0.0).astype(BF16) for h in range(GMLP_HEADS)]
    lane = lax.broadcasted_iota(jnp.int32, (rb, 128), 1)

    for rs in range(0, tm, rb):
        x = x_ref[rs:rs + rb, :]
        xn = _rmsnorm(x, g_ref[...]).astype(BF16)

        p = _dot(xn, win_ref[:, O_POOL:O_POOL + POOL_WIDTH])
        pbuf[POOL_HIST_PAD + rs:POOL_HIST_PAD + rs + rb, :] = p
        pos = (pos0 + t * tm + rs + lax.broadcasted_iota(jnp.int32, (rb, 1), 0) + 1).astype(F32)

        def prow(k, c):
            return pbuf[POOL_HIST_PAD + rs - k:POOL_HIST_PAD + rs - k + rb, c * 128:(c + 1) * 128]

        means = []
        for c in range(2):
            w_lo, w_hi = POOL_WINDOWS[2 * c], POOL_WINDOWS[2 * c + 1]
            s_lo = prow(0, c)
            for k in range(1, w_lo):
                s_lo = s_lo + prow(k, c)
            s_hi = s_lo
            for k in range(w_lo, w_hi):
                s_hi = s_hi + prow(k, c)
            m_lo = s_lo / jnp.minimum(float(w_lo), pos)
            m_hi = s_hi / jnp.minimum(float(w_hi), pos)
            means.append(jnp.where(lane < POOL_GDIM, m_lo, m_hi))
        d = (jnp.concatenate(means, axis=1) - p).astype(BF16)
        ya = _dot(d, wp_ref[...]) * sp_ref[...]
        merged = _sigmoid(_dot(xn, win_ref[:, O_GATE:O_GATE + D_MODEL]) + bg_ref[0:1, :]) * ya

        z = jax.nn.gelu(_dot(xn, win_ref[:, O_GMLP:O_GMLP + 2 * GMLP_WIDTH]))
        u = z[:, :GMLP_WIDTH]
        vn = _layernorm(z[:, GMLP_WIDTH:], ggv_ref[...], bgv_ref[...])
        if emit_vn:
            vn_ref[rs:rs + rb, :] = vn
        vb = vn.astype(BF16)
        chunks = [jnp.concatenate(
            [_dot(ws[h], vb[c0:c0 + cr, h * GMLP_HDIM:(h + 1) * GMLP_HDIM]) + bs_ref[0:cr, h:h + 1]
             for h in range(GMLP_HEADS)], axis=1) for c0 in range(0, rb, cr)]
        mixed = chunks[0] if len(chunks) == 1 else jnp.concatenate(chunks, axis=0)
        yb = _dot((u * mixed).astype(BF16), wgo_ref[...])
        merged = merged + _sigmoid(_dot(xn, win_ref[:, O_GATE + D_MODEL:O_GATE + 2 * D_MODEL])
                                   + bg_ref[1:2, :]) * yb

        cc = _dot(xn, win_ref[:, O_CONV:O_CONV + 2 * CONV_WIDTH])
        a = cc[:, :CONV_WIDTH] * _sigmoid(cc[:, CONV_WIDTH:])
        abuf[CONV_HIST_PAD + rs:CONV_HIST_PAD + rs + rb, :] = a
        c_lo = 0 if rs == 0 else rs + CONV_HIST_PAD - SUBLANES
        c_hi = rs + rb + CONV_HIST_PAD - SUBLANES
        for r in range(1, SUBLANES):
            sbuf[r - 1, c_lo:c_hi, :] = abuf[c_lo + r:c_hi + r, :]

        def tap(k, r0, n):
            q, r = divmod(CONV_HIST_PAD - CONV_HIST + k, SUBLANES)
            lo = SUBLANES * q + r0
            return abuf[lo:lo + n, :] if r == 0 else sbuf[r - 1, lo:lo + n, :]

        rc = min(CONV_ROWS, rb)
        hs = []
        for r0 in range(rs, rs + rb, rc):
            acc = tap(0, r0, rc) * wdw_ref[0:1, :]
            for k in range(1, CONV_K):
                acc = acc + tap(k, r0, rc) * wdw_ref[k:k + 1, :]
            hln = _layernorm(acc + bdw_ref[...], gcl_ref[...], bcl_ref[...])
            hs.append((hln * _sigmoid(hln)).astype(BF16))
        yc = _dot(hs[0] if len(hs) == 1 else jnp.concatenate(hs, axis=0), wco_ref[...])
        merged = merged + _sigmoid(_dot(xn, win_ref[:, O_GATE + 2 * D_MODEL:O_GATE + 3 * D_MODEL])
                                   + bg_ref[2:3, :]) * yc

        out_ref[rs:rs + rb, :] = x + _dot(merged.astype(BF16), wo_ref[...])

    p_tail = pbuf[tm:tm + POOL_HIST_PAD, :]
    a_tail = abuf[tm:tm + CONV_HIST_PAD, :]
    pbuf[0:POOL_HIST_PAD, :] = p_tail
    abuf[0:CONV_HIST_PAD, :] = a_tail
    pt_ref[...] = p_tail
    at_ref[...] = a_tail


def _mixer(x, pool_hist, conv_hist, w, *, tm, pos0, emit_vn):
    B, T, D = x.shape
    assert T % tm == 0 and tm >= CONV_HIST_PAD and (tm % GMLP_CHUNK == 0 or tm == T)
    tok = pl.BlockSpec((None, tm, D), lambda b, t: (b, t, 0))
    per_b = lambda r, c: pl.BlockSpec((None, r, c), lambda b, t: (b, 0, 0))
    weights = [w['g_mix'], w['w_in'], w['b_gate'], w['w_pool_bd'], w['s_pool'], w['g_gv'], w['b_gv'],
               w['w_s'], w['b_s_t'], w['w_gmlp_out'], w['w_dw'], w['b_dw'], w['g_cln'], w['b_cln'],
               w['w_conv_out'], w['w_out']]
    out_shape = [jax.ShapeDtypeStruct((B, T, D), F32),
                 jax.ShapeDtypeStruct((B, POOL_HIST_PAD, POOL_WIDTH), F32),
                 jax.ShapeDtypeStruct((B, CONV_HIST_PAD, CONV_WIDTH), F32)]
    out_specs = [tok, per_b(POOL_HIST_PAD, POOL_WIDTH), per_b(CONV_HIST_PAD, CONV_WIDTH)]
    if emit_vn:
        out_shape.append(jax.ShapeDtypeStruct((B, T, GMLP_WIDTH), F32))
        out_specs.append(pl.BlockSpec((None, tm, GMLP_WIDTH), lambda b, t: (b, t, 0)))
    return pl.pallas_call(
        functools.partial(_mixer_kernel, tm=tm, pos0=pos0, emit_vn=emit_vn),
        grid=(B, T // tm),
        in_specs=[tok, per_b(POOL_HIST_PAD, POOL_WIDTH), per_b(CONV_HIST_PAD, CONV_WIDTH)]
                 + [_const_spec(a.shape) for a in weights],
        out_specs=out_specs,
        out_shape=out_shape,
        scratch_shapes=[pltpu.VMEM((POOL_HIST_PAD + tm, POOL_WIDTH), F32),
                        pltpu.VMEM((CONV_HIST_PAD + tm, CONV_WIDTH), F32),
                        pltpu.VMEM((SUBLANES - 1, CONV_HIST_PAD + tm - SUBLANES, CONV_WIDTH), F32)],
        compiler_params=pltpu.CompilerParams(dimension_semantics=("arbitrary", "arbitrary"),
                                             vmem_limit_bytes=VMEM_LIMIT_BYTES),
        name="mixer",
    )(x, pool_hist, conv_hist, *weights)


def _kv_kernel(m_ref, g_ref, wk_ref, wv_ref, k_ref, v_ref):
    mn = _rmsnorm(m_ref[...], g_ref[...]).astype(BF16)
    k_ref[...] = _dot(mn, wk_ref[...])
    v_ref[...] = _dot(mn, wv_ref[...])


def _memory_kv(mem, w, *, tm):
    B, M, D = mem.shape
    rows = mem.reshape(B * M, D)
    blk = pl.BlockSpec((tm, D), lambda i: (i, 0))
    k, v = pl.pallas_call(
        _kv_kernel,
        grid=(B * M // tm,),
        in_specs=[blk, _const_spec((1, D)), _const_spec((D, D)), _const_spec((D, D))],
        out_specs=[blk, blk],
        out_shape=[jax.ShapeDtypeStruct((B * M, D), F32)] * 2,
        compiler_params=pltpu.CompilerParams(dimension_semantics=("arbitrary",),
                                             vmem_limit_bytes=VMEM_LIMIT_BYTES),
        name="memory_kv",
    )(rows, w['g_mem'], w['w_xk'], w['w_xv'])
    return k.reshape(B, M, D), v.reshape(B, M, D)


def _xattn_kernel(x_ref, k_ref, v_ref, g_ref, wq_ref, wo_ref, *rest, route):
    if route:
        gf_ref, wrh_ref, wrl_ref, br_ref, out_ref, bkt_ref, kb, vb = rest
    else:
        out_ref, kb, vb = rest

    @pl.when(pl.program_id(1) == 0)
    def _():
        kb[...] = k_ref[...].astype(BF16)
        vb[...] = v_ref[...].astype(BF16)

    x = x_ref[...]
    xn = _rmsnorm(x, g_ref[...]).astype(BF16)
    q = (_dot(xn, wq_ref[...]) * (XA_HDIM ** -0.5)).astype(BF16)
    heads = []
    for h in range(XA_HEADS):
        sl = slice(h * XA_HDIM, (h + 1) * XA_HDIM)
        s = lax.dot_general(q[:, sl], kb[:, sl], (((1,), (1,)), ((), ())), preferred_element_type=F32)
        e = jnp.exp(s - jnp.max(s, axis=-1, keepdims=True))
        pr = (e / jnp.sum(e, axis=-1, keepdims=True)).astype(BF16)
        heads.append(_dot(pr, vb[:, sl]).astype(BF16))
    o = jnp.concatenate(heads, axis=1)
    y = x + _dot(o, wo_ref[...])
    out_ref[...] = y
    if route:
        lane, gsel, _, i1, i2, _, _ = _route_select(_rmsnorm(y, gf_ref[...]), wrh_ref, wrl_ref, br_ref)
        e_lo = LANE_E0 + gsel * EXP_PER_GROUP
        a = jnp.minimum(i1, i2) - e_lo
        b = jnp.maximum(i1, i2) - e_lo
        pair = jnp.where(a == 0, 0, jnp.where(a == 1, 3, 5)) + (b - a - 1)
        bkt_ref[...] = jnp.broadcast_to((gsel * PAIRS_PER_GROUP + pair).astype(F32), bkt_ref.shape)


def _cross_attn(x, k, v, w, *, tm, route):
    B, T, D = x.shape
    tok = pl.BlockSpec((None, tm, D), lambda b, t: (b, t, 0))
    mem = pl.BlockSpec((None, MEM_LEN, D), lambda b, t: (b, 0, 0))
    in_specs = [tok, mem, mem, _const_spec((1, D)), _const_spec((D, D)), _const_spec((D, D))]
    args = [x, k, v, w['g_xa'], w['w_xq'], w['w_xo']]
    out_specs = [tok]
    out_shape = [jax.ShapeDtypeStruct((B, T, D), F32)]
    if route:
        in_specs += [_const_spec((1, D)), _const_spec((D, ROUTER_LANES)), _const_spec((D, ROUTER_LANES)),
                     _const_spec((1, ROUTER_LANES))]
        args += [w['g_ffn'], w['wr_hi'], w['wr_lo'], w['b_r']]
        out_specs.append(pl.BlockSpec((None, tm, ROUTER_LANES), lambda b, t: (b, t, 0)))
        out_shape.append(jax.ShapeDtypeStruct((B, T, ROUTER_LANES), F32))
    res = pl.pallas_call(
        functools.partial(_xattn_kernel, route=route),
        grid=(B, T // tm),
        in_specs=in_specs,
        out_specs=out_specs,
        out_shape=out_shape,
        scratch_shapes=[pltpu.VMEM((MEM_LEN, D), BF16), pltpu.VMEM((MEM_LEN, D), BF16)],
        compiler_params=pltpu.CompilerParams(dimension_semantics=("arbitrary", "arbitrary"),
                                             vmem_limit_bytes=VMEM_LIMIT_BYTES),
        name="cross_attn_route" if route else "cross_attn",
    )(*args)
    return res if route else res[0]


def _route_select(xn, wr_hi_ref, wr_lo_ref, br_ref):
    x_hi = xn.astype(BF16)
    x_lo = (xn - x_hi.astype(F32)).astype(BF16)
    logits = (_dot(x_hi, wr_hi_ref[...]) + _dot(x_lo, wr_hi_ref[...]) + _dot(x_hi, wr_lo_ref[...])
              + br_ref[...])
    lane = lax.broadcasted_iota(jnp.int32, logits.shape, 1)
    neg = jnp.float32(-jnp.inf)
    big = jnp.int32(ROUTER_LANES)

    def first_argmax(vals):
        m = jnp.max(vals, axis=-1, keepdims=True)
        return m, jnp.min(jnp.where(vals == m, lane, big), axis=-1, keepdims=True)

    gl = jnp.where(lane < N_GROUPS, logits, neg)
    gmax, gsel = first_argmax(gl)
    pg = 1.0 / jnp.sum(jnp.exp(gl - gmax), axis=-1, keepdims=True)
    e_lo = LANE_E0 + gsel * EXP_PER_GROUP
    el = jnp.where((lane >= e_lo) & (lane < e_lo + EXP_PER_GROUP), logits, neg)
    v1, i1 = first_argmax(el)
    v2, i2 = first_argmax(jnp.where(lane == i1, neg, el))
    return lane, gsel, pg, i1, i2, v1, v2


def _route(xn, wr_hi_ref, wr_lo_ref, br_ref):
    lane, _, pg, i1, i2, v1, v2 = _route_select(xn, wr_hi_ref, wr_lo_ref, br_ref)
    r = jnp.exp(v2 - v1)
    pe1 = pg / (1.0 + r)
    pe2 = pg * r / (1.0 + r)
    return jnp.where(lane == i1, pe1, 0.0) + jnp.where(lane == i2, pe2, 0.0)


def _moe_dense_kernel(x_ref, g_ref, wrh_ref, wrl_ref, br_ref, wgu_ref, wd_ref, gf_ref, out_ref, he_buf,
                      *, final_norm):
    x = x_ref[...]
    xn = _rmsnorm(x, g_ref[...])
    comb = _route(xn, wrh_ref, wrl_ref, br_ref)
    xb = xn.astype(BF16)
    lane = lax.broadcasted_iota(jnp.int32, comb.shape, 1)
    for e in range(N_EXPERTS):
        c = jnp.sum(jnp.where(lane == e + LANE_E0, comb, 0.0), axis=-1, keepdims=True)
        h = _dot(xb, wgu_ref[e])
        hg = h[:, :D_EXPERT]
        he = c * (hg * _sigmoid(hg) * h[:, D_EXPERT:])
        he_buf[:, e * D_EXPERT:(e + 1) * D_EXPERT] = he.astype(BF16)
    y = x + _dot(he_buf[...], wd_ref[...])
    out_ref[...] = _rmsnorm(y, gf_ref[...]) if final_norm else y


def _moe_dense(x, w, g_final, *, tm, final_norm):
    B, T, D = x.shape
    rows = x.reshape(B * T, D)
    tok = pl.BlockSpec((tm, D), lambda i: (i, 0))
    out = pl.pallas_call(
        functools.partial(_moe_dense_kernel, final_norm=final_norm),
        grid=(B * T // tm,),
        in_specs=[tok, _const_spec((1, D)), _const_spec((D, ROUTER_LANES)), _const_spec((D, ROUTER_LANES)),
                  _const_spec((1, ROUTER_LANES)), _const_spec((N_EXPERTS, D, 2 * D_EXPERT)),
                  _const_spec((N_EXPERTS * D_EXPERT, D)), _const_spec((1, D))],
        out_specs=tok,
        out_shape=jax.ShapeDtypeStruct((B * T, D), F32),
        scratch_shapes=[pltpu.VMEM((tm, N_EXPERTS * D_EXPERT), BF16)],
        compiler_params=pltpu.CompilerParams(dimension_semantics=("arbitrary",),
                                             vmem_limit_bytes=VMEM_LIMIT_BYTES),
        name="moe_dense",
    )(rows, w['g_ffn'], w['wr_hi'], w['wr_lo'], w['b_r'], jnp.concatenate([w['w_eg'], w['w_eu']], axis=-1),
      w['w_ed'].reshape(N_EXPERTS * D_EXPERT, D), g_final)
    return out.reshape(B, T, D)


def _moe_sorted_kernel(tile_ref, ea_ref, eb_ref, grp_ref, lo_ref, hi_ref, flag_ref,
                       sprev_ref, scur_ref, snext_ref, x_hbm, g_ref, wrh_ref, br_ref,
                       wga_ref, wua_ref, wda_ref, wgb_ref, wub_ref, wdb_ref, gf_ref,
                       y_hbm, xb0, xb1, ob0, ob1, gsem, ssem, *, tmg, n_tiles, final_norm):
    v = pl.program_id(0)
    tile, lo, hi, flags = tile_ref[v], lo_ref[v], hi_ref[v], flag_ref[v]
    xb, ob = (xb0, xb1), (ob0, ob1)
    first = (flags & 1) == 1
    last = (flags & 2) == 2
    has_next = tile < n_tiles - 1
    has_prev = tile > 0
    p_last = (n_tiles - 1) % 2

    def gather_start(idx_ref, k, p):
        pltpu.make_async_copy(x_hbm.at[pl.ds(idx_ref[0, k], 1)], xb[p].at[pl.ds(k, 1)], gsem.at[p]).start()

    def scatter_start(idx_ref, k, p):
        pltpu.make_async_copy(ob[p].at[pl.ds(k, 1)], y_hbm.at[pl.ds(idx_ref[0, k], 1)], ssem.at[p]).start()

    def gather_wait(p):
        pltpu.make_async_copy(x_hbm.at[pl.ds(0, tmg)], xb[p], gsem.at[p]).wait()

    def scatter_wait(p):
        pltpu.make_async_copy(ob[p], y_hbm.at[pl.ds(0, tmg)], ssem.at[p]).wait()

    def on_parity(cond, fn):
        for p in (0, 1):
            pl.when(cond & (tile % 2 == p))(functools.partial(fn, p))

    @pl.when(v == 0)
    def _():
        for k in range(tmg):
            gather_start(scur_ref, k, 0)

    def begin_tile(p):
        gather_wait(p)

        @pl.when(tile >= 2)
        def _():
            scatter_wait(p)

        ob[p][...] = jnp.zeros((tmg, D_MODEL), F32)

    on_parity(first, begin_tile)

    def compute(overlap_dma, p):
        if overlap_dma:
            for k in range(tmg):
                gather_start(snext_ref, k, 1 - p)
            for k in range(tmg):
                scatter_start(sprev_ref, k, 1 - p)
        xn = _rmsnorm(xb[p][...], g_ref[...]).astype(BF16)
        logits = _dot(xn, wrh_ref[...]) + br_ref[...]
        lane = lax.broadcasted_iota(jnp.int32, logits.shape, 1)
        pick = lambda l: jnp.sum(jnp.where(lane == l, logits, 0.0), axis=-1, keepdims=True)
        gl = jnp.where(lane < N_GROUPS, logits, jnp.float32(-jnp.inf))
        gmax = jnp.max(gl, axis=-1, keepdims=True)
        pg = jnp.exp(pick(grp_ref[v]) - gmax) / jnp.sum(jnp.exp(gl - gmax), axis=-1, keepdims=True)
        va, vb = pick(LANE_E0 + ea_ref[v]), pick(LANE_E0 + eb_ref[v])
        row = lax.broadcasted_iota(jnp.int32, (tmg, 1), 0)
        inside = (row >= lo) & (row < hi)
        acc = ob[p][...]
        for wt, wg_ref, wu_ref, wd_ref in ((pg / (1.0 + jnp.exp(vb - va)), wga_ref, wua_ref, wda_ref),
                                           (pg / (1.0 + jnp.exp(va - vb)), wgb_ref, wub_ref, wdb_ref)):
            hg = _dot(xn, wg_ref[...])
            he = jnp.where(inside, wt * (hg * _sigmoid(hg) * _dot(xn, wu_ref[...])), 0.0)
            acc = acc + _dot(he.astype(BF16), wd_ref[...])
        ob[p][...] = acc

    interior = first & has_next & has_prev
    on_parity(interior, functools.partial(compute, True))

    @pl.when(first & has_next & ~has_prev)
    def _():
        for k in range(tmg):
            gather_start(snext_ref, k, 1)

    @pl.when(first & ~has_next & has_prev)
    def _():
        for k in range(tmg):
            scatter_start(sprev_ref, k, 1 - p_last)

    on_parity((first & ~interior) | (~first & (hi > lo)), functools.partial(compute, False))

    def end_tile(p):
        y = xb[p][...] + ob[p][...]
        ob[p][...] = _rmsnorm(y, gf_ref[...]) if final_norm else y

    on_parity(last, end_tile)

    @pl.when(v == pl.num_programs(0) - 1)
    def _():
        for k in range(tmg):
            scatter_start(scur_ref, k, p_last)
        if n_tiles > 1:
            scatter_wait(1 - p_last)
        scatter_wait(p_last)


def _moe_plan(bucket, n, tmg):
    i32 = jnp.int32
    src = jnp.argsort(bucket, stable=True).astype(i32)
    cnt = jnp.sum((bucket[:, None] == jnp.arange(N_BUCKETS, dtype=i32)[None, :]).astype(i32), axis=0)
    ends = jnp.cumsum(cnt)
    offs = ends - cnt
    nt = n // tmg
    cuts = jnp.sort(jnp.concatenate([jnp.arange(nt, dtype=i32) * tmg, offs[1:]]))
    nxt = jnp.concatenate([cuts[1:], jnp.full((1,), n, i32)])
    tile = jnp.minimum(cuts // tmg, nt - 1)
    bkt = jnp.minimum(jnp.sum((ends[None, :] <= cuts[:, None]).astype(i32), axis=1), N_BUCKETS - 1)
    change = (tile[1:] != tile[:-1]).astype(i32)
    one = jnp.ones((1,), i32)
    flags = jnp.concatenate([one, change]) + 2 * jnp.concatenate([change, one])
    grp = bkt // PAIRS_PER_GROUP
    pair = bkt % PAIRS_PER_GROUP
    ea = grp * EXP_PER_GROUP + jnp.take(jnp.array([0, 0, 0, 1, 1, 2], i32), pair)
    eb = grp * EXP_PER_GROUP + jnp.take(jnp.array([1, 2, 3, 2, 3, 3], i32), pair)
    return src, (tile, ea, eb, grp, cuts - tile * tmg, nxt - tile * tmg, flags)


def _moe_sorted(x, bucket, w, g_final, *, tmg, final_norm):
    B, T, D = x.shape
    n = B * T
    nt = n // tmg
    assert n % tmg == 0
    src, tables = _moe_plan(bucket.reshape(n, ROUTER_LANES)[:, 0].astype(jnp.int32), n, tmg)
    src3 = src.reshape(nt, 1, tmg)
    idx = lambda d: pl.BlockSpec((None, 1, tmg), lambda v, tile, *_: (jnp.clip(tile[v] + d, 0, nt - 1), 0, 0),
                                 memory_space=pltpu.SMEM)
    const = lambda shape: pl.BlockSpec(shape, lambda v, *_: (0,) * len(shape), pipeline_mode=pl.Buffered(1))
    w_a = lambda r, c: pl.BlockSpec((None, r, c), lambda v, tile, ea, eb, *_: (ea[v], 0, 0))
    w_b = lambda r, c: pl.BlockSpec((None, r, c), lambda v, tile, ea, eb, *_: (eb[v], 0, 0))
    any_spec = pl.BlockSpec(memory_space=pl.ANY)
    y = pl.pallas_call(
        functools.partial(_moe_sorted_kernel, tmg=tmg, n_tiles=nt, final_norm=final_norm),
        grid_spec=pltpu.PrefetchScalarGridSpec(
            num_scalar_prefetch=len(tables),
            grid=(tables[0].shape[0],),
            in_specs=[idx(-1), idx(0), idx(1), any_spec, const((1, D)), const((D, ROUTER_LANES)),
                      const((1, ROUTER_LANES)),
                      w_a(D, D_EXPERT), w_a(D, D_EXPERT), w_a(D_EXPERT, D),
                      w_b(D, D_EXPERT), w_b(D, D_EXPERT), w_b(D_EXPERT, D), const((1, D))],
            out_specs=any_spec,
            scratch_shapes=[pltpu.VMEM((tmg, D), F32)] * 4
                           + [pltpu.SemaphoreType.DMA((2,)), pltpu.SemaphoreType.DMA((2,))]),
        out_shape=jax.ShapeDtypeStruct((n, D), F32),
        compiler_params=pltpu.CompilerParams(dimension_semantics=("arbitrary",),
                                             vmem_limit_bytes=VMEM_LIMIT_BYTES, has_side_effects=True),
        name="moe_sorted",
    )(*tables, src3, src3, src3, x.reshape(n, D), w['g_ffn'], w['wr_hi'], w['b_r'],
      w['w_eg'], w['w_eu'], w['w_ed'], w['w_eg'], w['w_eu'], w['w_ed'], g_final)
    return y.reshape(B, T, D)


def _layer_weights(l, g_mix, w_in, b_gate, w_pool, s_pool, g_gv, b_gv, w_s, b_s, w_gmlp_out,
                   w_dw, b_dw, g_cln, b_cln, w_conv_out, w_out, g_xa, g_mem, w_xq, w_xk, w_xv, w_xo,
                   g_ffn, w_rg, b_rg, w_re, b_re, w_eg, w_eu, w_ed):
    row = lambda a: a[l].reshape(1, -1)
    wp = jnp.zeros((POOL_WIDTH, D_MODEL), F32)
    gout = D_MODEL // len(POOL_WINDOWS)
    for g in range(len(POOL_WINDOWS)):
        wp = wp.at[g * POOL_GDIM:(g + 1) * POOL_GDIM, g * gout:(g + 1) * gout].set(w_pool[l, g])
    wr = jnp.zeros((D_MODEL, ROUTER_LANES), F32)
    wr = wr.at[:, :N_GROUPS].set(w_rg[l]).at[:, LANE_E0:LANE_E0 + N_EXPERTS].set(w_re[l])
    wr_hi = wr.astype(BF16)
    br = jnp.zeros((1, ROUTER_LANES), F32)
    br = br.at[0, :N_GROUPS].set(b_rg[l]).at[0, LANE_E0:LANE_E0 + N_EXPERTS].set(b_re[l])
    return {
        'g_mix': row(g_mix), 'w_in': w_in[l].astype(BF16), 'b_gate': b_gate[l],
        'w_pool_bd': wp.astype(BF16), 's_pool': row(s_pool), 'g_gv': row(g_gv), 'b_gv': row(b_gv),
        'w_s': w_s[l], 'b_s_t': jnp.transpose(b_s[l]), 'w_gmlp_out': w_gmlp_out[l].astype(BF16),
        'w_dw': w_dw[l], 'b_dw': row(b_dw), 'g_cln': row(g_cln), 'b_cln': row(b_cln),
        'w_conv_out': w_conv_out[l].astype(BF16), 'w_out': w_out[l].astype(BF16),
        'g_xa': row(g_xa), 'g_mem': row(g_mem), 'w_xq': w_xq[l].astype(BF16),
        'w_xk': w_xk[l].astype(BF16), 'w_xv': w_xv[l].astype(BF16), 'w_xo': w_xo[l].astype(BF16),
        'g_ffn': row(g_ffn), 'wr_hi': wr_hi, 'wr_lo': (wr - wr_hi.astype(F32)).astype(BF16), 'b_r': br,
        'w_eg': w_eg[l].astype(BF16), 'w_eu': w_eu[l].astype(BF16), 'w_ed': w_ed[l].astype(BF16),
    }


def _attn_moe(x, k, v, w, g_final, *, tm, final_norm):
    n = x.shape[0] * x.shape[1]
    if n >= MOE_SORTED_MIN_ROWS:
        x, bucket = _cross_attn(x, k, v, w, tm=tm, route=True)
        return _moe_sorted(x, bucket, w, g_final, tmg=MOE_SORTED_TILE, final_norm=final_norm)
    x = _cross_attn(x, k, v, w, tm=tm, route=False)
    return _moe_dense(x, w, g_final, tm=_token_tile(n), final_norm=final_norm)


def _pad_hist(h, rows):
    return jnp.pad(h, ((0, 0), (rows - h.shape[1], 0), (0, 0)))


def _token_tile(t):
    return min(512, t)


def kernel(x_prompt, x_sample, mem_prompt, cache_pool, cache_conv, cache_mem_k, cache_mem_v, g_mix, w_in, b_gate, w_pool, s_pool, g_gv, b_gv, w_s, b_s, w_gmlp_out, w_dw, b_dw, g_cln, b_cln, w_conv_out, w_out, g_xa, g_mem, w_xq, w_xk, w_xv, w_xo, g_ffn, w_rg, b_rg, w_re, b_re, w_eg, w_eu, w_ed, g_final):
    depth = w_in.shape[0]
    bp, tp, _ = x_prompt.shape
    bs, ts, _ = x_sample.shape
    tmp, tms = _token_tile(tp), _token_tile(ts)
    gf = g_final.reshape(1, -1)
    xp, xs = x_prompt, x_sample
    pool0 = jnp.zeros((bp, POOL_HIST_PAD, POOL_WIDTH), F32)
    conv0 = jnp.zeros((bp, CONV_HIST_PAD, CONV_WIDTH), F32)
    outs = {k: [] for k in ('pp', 'pc', 'pk', 'pv', 'sp', 'sc', 'sv')}
    for l in range(depth):
        w = _layer_weights(l, g_mix, w_in, b_gate, w_pool, s_pool, g_gv, b_gv, w_s, b_s, w_gmlp_out,
                           w_dw, b_dw, g_cln, b_cln, w_conv_out, w_out, g_xa, g_mem, w_xq, w_xk, w_xv,
                           w_xo, g_ffn, w_rg, b_rg, w_re, b_re, w_eg, w_eu, w_ed)
        last = l == depth - 1
        xp, ph, ch = _mixer(xp, pool0, conv0, w, tm=tmp, pos0=0, emit_vn=False)
        mk, mv = _memory_kv(mem_prompt, w, tm=_token_tile(bp * MEM_LEN))
        xp = _attn_moe(xp, mk, mv, w, gf, tm=tmp, final_norm=last)
        outs['pp'].append(ph[:, POOL_HIST_PAD - POOL_HIST:])
        outs['pc'].append(ch[:, CONV_HIST_PAD - CONV_HIST:])
        outs['pk'].append(mk.reshape(bp, MEM_LEN, XA_HEADS, XA_HDIM))
        outs['pv'].append(mv.reshape(bp, MEM_LEN, XA_HEADS, XA_HDIM))
        xs, sh, sc, vn = _mixer(xs, _pad_hist(cache_pool[l], POOL_HIST_PAD),
                                _pad_hist(cache_conv[l], CONV_HIST_PAD), w,
                                tm=tms, pos0=PAST_LEN, emit_vn=True)
        xs = _attn_moe(xs, cache_mem_k[l].reshape(bs, MEM_LEN, D_MODEL),
                       cache_mem_v[l].reshape(bs, MEM_LEN, D_MODEL), w, gf, tm=tms, final_norm=last)
        outs['sp'].append(sh[:, POOL_HIST_PAD - POOL_HIST:])
        outs['sc'].append(sc[:, CONV_HIST_PAD - CONV_HIST:])
        outs['sv'].append(vn)
    st = lambda k: jnp.stack(outs[k], axis=0)
    return (xp, xs, st('pp'), st('pc'), st('pk'), st('pv'), st('sp'), st('sc'), st('sv'))
```

```python
import functools

import jax
import jax.numpy as jnp
from jax import lax
from jax.experimental import pallas as pl
from jax.experimental.pallas import tpu as pltpu

F32 = jnp.float32
BF16 = jnp.bfloat16

EPS = 1e-6
D_MODEL = 1024
PAST_LEN = 4096
CHUNK = 64
POOL_WIDTH = 256
POOL_WINDOWS = (2, 4, 8, 16)
POOL_GDIM = 64
POOL_HIST = 15
POOL_HIST_PAD = 16
GMLP_WIDTH = 512
GMLP_HEADS = 4
GMLP_HDIM = 128
GMLP_CHUNK = 128
CONV_WIDTH = 256
CONV_K = 31
CONV_HIST = 30
CONV_HIST_PAD = 32
SUBLANES = 8
MIXER_ROW_BLOCK = 512
CONV_ROWS = 64
N_BRANCH = 3
O_POOL = 0
O_GMLP = POOL_WIDTH
O_CONV = O_GMLP + 2 * GMLP_WIDTH
O_GATE = O_CONV + 2 * CONV_WIDTH
IN_COLS = O_GATE + N_BRANCH * D_MODEL
MEM_LEN = 256
XA_HEADS = 4
XA_HDIM = 256
N_GROUPS = 4
EXP_PER_GROUP = 4
N_EXPERTS = 16
D_EXPERT = 256
ROUTER_LANES = 128
LANE_E0 = N_GROUPS
PAIRS_PER_GROUP = 6
N_BUCKETS = N_GROUPS * PAIRS_PER_GROUP
MOE_SORTED_TILE = 256
MOE_SORTED_MIN_ROWS = 1024

VMEM_LIMIT_BYTES = 56 * 1024 * 1024


def _dot(a, b):
    return jnp.dot(a, b, preferred_element_type=F32)


def _rmsnorm(x, g):
    return x * lax.rsqrt(jnp.mean(x * x, axis=-1, keepdims=True) + EPS) * g


def _layernorm(x, g, b):
    mu = jnp.mean(x, axis=-1, keepdims=True)
    xc = x - mu
    var = jnp.mean(xc * xc, axis=-1, keepdims=True)
    return xc * lax.rsqrt(var + EPS) * g + b


def _sigmoid(x):
    return 1.0 / (1.0 + jnp.exp(-x))


def _const_spec(shape):
    nd = len(shape)
    return pl.BlockSpec(shape, lambda *_: (0,) * nd, pipeline_mode=pl.Buffered(1))


def _mixer_kernel(x_ref, ph_ref, ch_ref, g_ref, win_ref, bg_ref, wp_ref, sp_ref, ggv_ref, bgv_ref,
                  ws_ref, bs_ref, wgo_ref, wdw_ref, bdw_ref, gcl_ref, bcl_ref, wco_ref, wo_ref,
                  *rest, tm, pos0, emit_vn):
    if emit_vn:
        out_ref, pt_ref, at_ref, vn_ref, pbuf, abuf, sbuf = rest
    else:
        out_ref, pt_ref, at_ref, pbuf, abuf, sbuf = rest
        vn_ref = None
    t = pl.program_id(1)

    @pl.when(t == 0)
    def _():
        pbuf[0:POOL_HIST_PAD, :] = ph_ref[...]
        abuf[0:CONV_HIST_PAD, :] = ch_ref[...]

    rb = min(MIXER_ROW_BLOCK, tm)
    cr = min(GMLP_CHUNK, tm)
    bi = lax.broadcasted_iota(jnp.int32, (cr, cr), 0) // CHUNK
    bj = lax.broadcasted_iota(jnp.int32, (cr, cr), 1) // CHUNK
    ws = [jnp.where(bi >= bj, ws_ref[h, 0:cr, 0:cr], 0.0).astype(BF16) for h in range(GMLP_HEADS)]
    lane = lax.broadcasted_iota(jnp.int32, (rb, 128), 1)

    for rs in range(0, tm, rb):
        x = x_ref[rs:rs + rb, :]
        xn = _rmsnorm(x, g_ref[...]).astype(BF16)

        p = _dot(xn, win_ref[:, O_POOL:O_POOL + POOL_WIDTH])
        pbuf[POOL_HIST_PAD + rs:POOL_HIST_PAD + rs + rb, :] = p
        pos = (pos0 + t * tm + rs + lax.broadcasted_iota(jnp.int32, (rb, 1), 0) + 1).astype(F32)

        def prow(k, c):
            return pbuf[POOL_HIST_PAD + rs - k:POOL_HIST_PAD + rs - k + rb, c * 128:(c + 1) * 128]

        means = []
        for c in range(2):
            w_lo, w_hi = POOL_WINDOWS[2 * c], POOL_WINDOWS[2 * c + 1]
            s_lo = prow(0, c)
            for k in range(1, w_lo):
                s_lo = s_lo + prow(k, c)
            s_hi = s_lo
            for k in range(w_lo, w_hi):
                s_hi = s_hi + prow(k, c)
            m_lo = s_lo / jnp.minimum(float(w_lo), pos)
            m_hi = s_hi / jnp.minimum(float(w_hi), pos)
            means.append(jnp.where(lane < POOL_GDIM, m_lo, m_hi))
        d = (jnp.concatenate(means, axis=1) - p).astype(BF16)
        ya = _dot(d, wp_ref[...]) * sp_ref[...]
        merged = _sigmoid(_dot(xn, win_ref[:, O_GATE:O_GATE + D_MODEL]) + bg_ref[0:1, :]) * ya

        z = jax.nn.gelu(_dot(xn, win_ref[:, O_GMLP:O_GMLP + 2 * GMLP_WIDTH]))
        u = z[:, :GMLP_WIDTH]
        vn = _layernorm(z[:, GMLP_WIDTH:], ggv_ref[...], bgv_ref[...])
        if emit_vn:
            vn_ref[rs:rs + rb, :] = vn
        vb = vn.astype(BF16)
        chunks = [jnp.concatenate(
            [_dot(ws[h], vb[c0:c0 + cr, h * GMLP_HDIM:(h + 1) * GMLP_HDIM]) + bs_ref[0:cr, h:h + 1]
             for h in range(GMLP_HEADS)], axis=1) for c0 in range(0, rb, cr)]
        mixed = chunks[0] if len(chunks) == 1 else jnp.concatenate(chunks, axis=0)
        yb = _dot((u * mixed).astype(BF16), wgo_ref[...])
        merged = merged + _sigmoid(_dot(xn, win_ref[:, O_GATE + D_MODEL:O_GATE + 2 * D_MODEL])
                                   + bg_ref[1:2, :]) * yb

        cc = _dot(xn, win_ref[:, O_CONV:O_CONV + 2 * CONV_WIDTH])
        a = cc[:, :CONV_WIDTH] * _sigmoid(cc[:, CONV_WIDTH:])
        abuf[CONV_HIST_PAD + rs:CONV_HIST_PAD + rs + rb, :] = a
        c_lo = 0 if rs == 0 else rs + CONV_HIST_PAD - SUBLANES
        c_hi = rs + rb + CONV_HIST_PAD - SUBLANES
        for r in range(1, SUBLANES):
            sbuf[r - 1, c_lo:c_hi, :] = abuf[c_lo + r:c_hi + r, :]

        def tap(k, r0, n):
            q, r = divmod(CONV_HIST_PAD - CONV_HIST + k, SUBLANES)
            lo = SUBLANES * q + r0
            return abuf[lo:lo + n, :] if r == 0 else sbuf[r - 1, lo:lo + n, :]

        rc = min(CONV_ROWS, rb)
        hs = []
        for r0 in range(rs, rs + rb, rc):
            acc = tap(0, r0, rc) * wdw_ref[0:1, :]
            for k in range(1, CONV_K):
                acc = acc + tap(k, r0, rc) * wdw_ref[k:k + 1, :]
            hln = _layernorm(acc + bdw_ref[...], gcl_ref[...], bcl_ref[...])
            hs.append((hln * _sigmoid(hln)).astype(BF16))
        yc = _dot(hs[0] if len(hs) == 1 else jnp.concatenate(hs, axis=0), wco_ref[...])
        merged = merged + _sigmoid(_dot(xn, win_ref[:, O_GATE + 2 * D_MODEL:O_GATE + 3 * D_MODEL])
                                   + bg_ref[2:3, :]) * yc

        out_ref[rs:rs + rb, :] = x + _dot(merged.astype(BF16), wo_ref[...])

    p_tail = pbuf[tm:tm + POOL_HIST_PAD, :]
    a_tail = abuf[tm:tm + CONV_HIST_PAD, :]
    pbuf[0:POOL_HIST_PAD, :] = p_tail
    abuf[0:CONV_HIST_PAD, :] = a_tail
    pt_ref[...] = p_tail
    at_ref[...] = a_tail


def _mixer(x, pool_hist, conv_hist, w, *, tm, pos0, emit_vn):
    B, T, D = x.shape
    assert T % tm == 0 and tm >= CONV_HIST_PAD and (tm % GMLP_CHUNK == 0 or tm == T)
    tok = pl.BlockSpec((None, tm, D), lambda b, t: (b, t, 0))
    per_b = lambda r, c: pl.BlockSpec((None, r, c), lambda b, t: (b, 0, 0))
    weights = [w['g_mix'], w['w_in'], w['b_gate'], w['w_pool_bd'], w['s_pool'], w['g_gv'], w['b_gv'],
               w['w_s'], w['b_s_t'], w['w_gmlp_out'], w['w_dw'], w['b_dw'], w['g_cln'], w['b_cln'],
               w['w_conv_out'], w['w_out']]
    out_shape = [jax.ShapeDtypeStruct((B, T, D), F32),
                 jax.ShapeDtypeStruct((B, POOL_HIST_PAD, POOL_WIDTH), F32),
                 jax.ShapeDtypeStruct((B, CONV_HIST_PAD, CONV_WIDTH), F32)]
    out_specs = [tok, per_b(POOL_HIST_PAD, POOL_WIDTH), per_b(CONV_HIST_PAD, CONV_WIDTH)]
    if emit_vn:
        out_shape.append(jax.ShapeDtypeStruct((B, T, GMLP_WIDTH), F32))
        out_specs.append(pl.BlockSpec((None, tm, GMLP_WIDTH), lambda b, t: (b, t, 0)))
    return pl.pallas_call(
        functools.partial(_mixer_kernel, tm=tm, pos0=pos0, emit_vn=emit_vn),
        grid=(B, T // tm),
        in_specs=[tok, per_b(POOL_HIST_PAD, POOL_WIDTH), per_b(CONV_HIST_PAD, CONV_WIDTH)]
                 + [_const_spec(a.shape) for a in weights],
        out_specs=out_specs,
        out_shape=out_shape,
        scratch_shapes=[pltpu.VMEM((POOL_HIST_PAD + tm, POOL_WIDTH), F32),
                        pltpu.VMEM((CONV_HIST_PAD + tm, CONV_WIDTH), F32),
                        pltpu.VMEM((SUBLANES - 1, CONV_HIST_PAD + tm - SUBLANES, CONV_WIDTH), F32)],
        compiler_params=pltpu.CompilerParams(dimension_semantics=("arbitrary", "arbitrary"),
                                             vmem_limit_bytes=VMEM_LIMIT_BYTES),
        name="mixer",
    )(x, pool_hist, conv_hist, *weights)


def _kv_kernel(m_ref, g_ref, wk_ref, wv_ref, k_ref, v_ref):
    mn = _rmsnorm(m_ref[...], g_ref[...]).astype(BF16)
    k_ref[...] = _dot(mn, wk_ref[...])
    v_ref[...] = _dot(mn, wv_ref[...])


def _memory_kv(mem, w, *, tm):
    B, M, D = mem.shape
    rows = mem.reshape(B * M, D)
    blk = pl.BlockSpec((tm, D), lambda i: (i, 0))
    k, v = pl.pallas_call(
        _kv_kernel,
        grid=(B * M // tm,),
        in_specs=[blk, _const_spec((1, D)), _const_spec((D, D)), _const_spec((D, D))],
        out_specs=[blk, blk],
        out_shape=[jax.ShapeDtypeStruct((B * M, D), F32)] * 2,
        compiler_params=pltpu.CompilerParams(dimension_semantics=("arbitrary",),
                                             vmem_limit_bytes=VMEM_LIMIT_BYTES),
        name="memory_kv",
    )(rows, w['g_mem'], w['w_xk'], w['w_xv'])
    return k.reshape(B, M, D), v.reshape(B, M, D)


def _xattn_kernel(x_ref, k_ref, v_ref, g_ref, wq_ref, wo_ref, *rest, route):
    if route:
        gf_ref, wrh_ref, wrl_ref, br_ref, out_ref, bkt_ref, kb, vb = rest
    else:
        out_ref, kb, vb = rest

    @pl.when(pl.program_id(1) == 0)
    def _():
        kb[...] = k_ref[...].astype(BF16)
        vb[...] = v_ref[...].astype(BF16)

    x = x_ref[...]
    xn = _rmsnorm(x, g_ref[...]).astype(BF16)
    q = (_dot(xn, wq_ref[...]) * (XA_HDIM ** -0.5)).astype(BF16)
    heads = []
    for h in range(XA_HEADS):
        sl = slice(h * XA_HDIM, (h + 1) * XA_HDIM)
        s = lax.dot_general(q[:, sl], kb[:, sl], (((1,), (1,)), ((), ())), preferred_element_type=F32)
        e = jnp.exp(s - jnp.max(s, axis=-1, keepdims=True))
        pr = (e / jnp.sum(e, axis=-1, keepdims=True)).astype(BF16)
        heads.append(_dot(pr, vb[:, sl]).astype(BF16))
    o = jnp.concatenate(heads, axis=1)
    y = x + _dot(o, wo_ref[...])
    out_ref[...] = y
    if route:
        lane, gsel, _, i1, i2, _, _ = _route_select(_rmsnorm(y, gf_ref[...]), wrh_ref, wrl_ref, br_ref)
        e_lo = LANE_E0 + gsel * EXP_PER_GROUP
        a = jnp.minimum(i1, i2) - e_lo
        b = jnp.maximum(i1, i2) - e_lo
        pair = jnp.where(a == 0, 0, jnp.where(a == 1, 3, 5)) + (b - a - 1)
        bkt_ref[...] = jnp.broadcast_to((gsel * PAIRS_PER_GROUP + pair).astype(F32), bkt_ref.shape)


def _cross_attn(x, k, v, w, *, tm, route):
    B, T, D = x.shape
    tok = pl.BlockSpec((None, tm, D), lambda b, t: (b, t, 0))
    mem = pl.BlockSpec((None, MEM_LEN, D), lambda b, t: (b, 0, 0))
    in_specs = [tok, mem, mem, _const_spec((1, D)), _const_spec((D, D)), _const_spec((D, D))]
    args = [x, k, v, w['g_xa'], w['w_xq'], w['w_xo']]
    out_specs = [tok]
    out_shape = [jax.ShapeDtypeStruct((B, T, D), F32)]
    if route:
        in_specs += [_const_spec((1, D)), _const_spec((D, ROUTER_LANES)), _const_spec((D, ROUTER_LANES)),
                     _const_spec((1, ROUTER_LANES))]
        args += [w['g_ffn'], w['wr_hi'], w['wr_lo'], w['b_r']]
        out_specs.append(pl.BlockSpec((None, tm, ROUTER_LANES), lambda b, t: (b, t, 0)))
        out_shape.append(jax.ShapeDtypeStruct((B, T, ROUTER_LANES), F32))
    res = pl.pallas_call(
        functools.partial(_xattn_kernel, route=route),
        grid=(B, T // tm),
        in_specs=in_specs,
        out_specs=out_specs,
        out_shape=out_shape,
        scratch_shapes=[pltpu.VMEM((MEM_LEN, D), BF16), pltpu.VMEM((MEM_LEN, D), BF16)],
        compiler_params=pltpu.CompilerParams(dimension_semantics=("arbitrary", "arbitrary"),
                                             vmem_limit_bytes=VMEM_LIMIT_BYTES),
        name="cross_attn_route" if route else "cross_attn",
    )(*args)
    return res if route else res[0]


def _route_select(xn, wr_hi_ref, wr_lo_ref, br_ref):
    x_hi = xn.astype(BF16)
    x_lo = (xn - x_hi.astype(F32)).astype(BF16)
    logits = (_dot(x_hi, wr_hi_ref[...]) + _dot(x_lo, wr_hi_ref[...]) + _dot(x_hi, wr_lo_ref[...])
              + br_ref[...])
    lane = lax.broadcasted_iota(jnp.int32, logits.shape, 1)
    neg = jnp.float32(-jnp.inf)
    big = jnp.int32(ROUTER_LANES)

    def first_argmax(vals):
        m = jnp.max(vals, axis=-1, keepdims=True)
        return m, jnp.min(jnp.where(vals == m, lane, big), axis=-1, keepdims=True)

    gl = jnp.where(lane < N_GROUPS, logits, neg)
    gmax, gsel = first_argmax(gl)
    pg = 1.0 / jnp.sum(jnp.exp(gl - gmax), axis=-1, keepdims=True)
    e_lo = LANE_E0 + gsel * EXP_PER_GROUP
    el = jnp.where((lane >= e_lo) & (lane < e_lo + EXP_PER_GROUP), logits, neg)
    v1, i1 = first_argmax(el)
    v2, i2 = first_argmax(jnp.where(lane == i1, neg, el))
    return lane, gsel, pg, i1, i2, v1, v2


def _route(xn, wr_hi_ref, wr_lo_ref, br_ref):
    lane, _, pg, i1, i2, v1, v2 = _route_select(xn, wr_hi_ref, wr_lo_ref, br_ref)
    r = jnp.exp(v2 - v1)
    pe1 = pg / (1.0 + r)
    pe2 = pg * r / (1.0 + r)
    return jnp.where(lane == i1, pe1, 0.0) + jnp.where(lane == i2, pe2, 0.0)


def _moe_dense_kernel(x_ref, g_ref, wrh_ref, wrl_ref, br_ref, wgu_ref, wd_ref, gf_ref, out_ref, he_buf,
                      *, final_norm):
    x = x_ref[...]
    xn = _rmsnorm(x, g_ref[...])
    comb = _route(xn, wrh_ref, wrl_ref, br_ref)
    xb = xn.astype(BF16)
    lane = lax.broadcasted_iota(jnp.int32, comb.shape, 1)
    for e in range(N_EXPERTS):
        c = jnp.sum(jnp.where(lane == e + LANE_E0, comb, 0.0), axis=-1, keepdims=True)
        h = _dot(xb, wgu_ref[e])
        hg = h[:, :D_EXPERT]
        he = c * (hg * _sigmoid(hg) * h[:, D_EXPERT:])
        he_buf[:, e * D_EXPERT:(e + 1) * D_EXPERT] = he.astype(BF16)
    y = x + _dot(he_buf[...], wd_ref[...])
    out_ref[...] = _rmsnorm(y, gf_ref[...]) if final_norm else y


def _moe_dense(x, w, g_final, *, tm, final_norm):
    B, T, D = x.shape
    rows = x.reshape(B * T, D)
    tok = pl.BlockSpec((tm, D), lambda i: (i, 0))
    out = pl.pallas_call(
        functools.partial(_moe_dense_kernel, final_norm=final_norm),
        grid=(B * T // tm,),
        in_specs=[tok, _const_spec((1, D)), _const_spec((D, ROUTER_LANES)), _const_spec((D, ROUTER_LANES)),
                  _const_spec((1, ROUTER_LANES)), _const_spec((N_EXPERTS, D, 2 * D_EXPERT)),
                  _const_spec((N_EXPERTS * D_EXPERT, D)), _const_spec((1, D))],
        out_specs=tok,
        out_shape=jax.ShapeDtypeStruct((B * T, D), F32),
        scratch_shapes=[pltpu.VMEM((tm, N_EXPERTS * D_EXPERT), BF16)],
        compiler_params=pltpu.CompilerParams(dimension_semantics=("arbitrary",),
                                             vmem_limit_bytes=VMEM_LIMIT_BYTES),
        name="moe_dense",
    )(rows, w['g_ffn'], w['wr_hi'], w['wr_lo'], w['b_r'], jnp.concatenate([w['w_eg'], w['w_eu']], axis=-1),
      w['w_ed'].reshape(N_EXPERTS * D_EXPERT, D), g_final)
    return out.reshape(B, T, D)


def _moe_sorted_kernel(tile_ref, ea_ref, eb_ref, grp_ref, lo_ref, hi_ref, flag_ref,
                       sprev_ref, scur_ref, snext_ref, x_hbm, g_ref, wrh_ref, br_ref,
                       wga_ref, wua_ref, wda_ref, wgb_ref, wub_ref, wdb_ref, gf_ref,
                       y_hbm, xb0, xb1, ob0, ob1, gsem, ssem, *, tmg, n_tiles, final_norm):
    v = pl.program_id(0)
    tile, lo, hi, flags = tile_ref[v], lo_ref[v], hi_ref[v], flag_ref[v]
    xb, ob = (xb0, xb1), (ob0, ob1)
    first = (flags & 1) == 1
    last = (flags & 2) == 2
    has_next = tile < n_tiles - 1
    has_prev = tile > 0
    p_last = (n_tiles - 1) % 2

    def gather_start(idx_ref, k, p):
        pltpu.make_async_copy(x_hbm.at[pl.ds(idx_ref[0, k], 1)], xb[p].at[pl.ds(k, 1)],
                              gsem.at[p]).start(priority=k % 2)

    def scatter_start(idx_ref, k, p):
        pltpu.make_async_copy(ob[p].at[pl.ds(k, 1)], y_hbm.at[pl.ds(idx_ref[0, k], 1)],
                              ssem.at[p]).start(priority=k % 2)

    def gather_wait(p):
        pltpu.make_async_copy(x_hbm.at[pl.ds(0, tmg)], xb[p], gsem.at[p]).wait()

    def scatter_wait(p):
        pltpu.make_async_copy(ob[p], y_hbm.at[pl.ds(0, tmg)], ssem.at[p]).wait()

    def on_parity(cond, fn):
        for p in (0, 1):
            pl.when(cond & (tile % 2 == p))(functools.partial(fn, p))

    @pl.when(v == 0)
    def _():
        for k in range(tmg):
            gather_start(scur_ref, k, 0)

    def begin_tile(p):
        gather_wait(p)

        @pl.when(tile >= 2)
        def _():
            scatter_wait(p)

        ob[p][...] = jnp.zeros((tmg, D_MODEL), F32)

    on_parity(first, begin_tile)

    def compute(overlap_dma, p):
        if overlap_dma:
            for k in range(tmg):
                gather_start(snext_ref, k, 1 - p)
            for k in range(tmg):
                scatter_start(sprev_ref, k, 1 - p)
        xn = _rmsnorm(xb[p][...], g_ref[...]).astype(BF16)
        logits = _dot(xn, wrh_ref[...]) + br_ref[...]
        lane = lax.broadcasted_iota(jnp.int32, logits.shape, 1)
        pick = lambda l: jnp.sum(jnp.where(lane == l, logits, 0.0), axis=-1, keepdims=True)
        gl = jnp.where(lane < N_GROUPS, logits, jnp.float32(-jnp.inf))
        gmax = jnp.max(gl, axis=-1, keepdims=True)
        pg = jnp.exp(pick(grp_ref[v]) - gmax) / jnp.sum(jnp.exp(gl - gmax), axis=-1, keepdims=True)
        va, vb = pick(LANE_E0 + ea_ref[v]), pick(LANE_E0 + eb_ref[v])
        row = lax.broadcasted_iota(jnp.int32, (tmg, 1), 0)
        inside = (row >= lo) & (row < hi)
        acc = ob[p][...]
        for wt, wg_ref, wu_ref, wd_ref in ((pg / (1.0 + jnp.exp(vb - va)), wga_ref, wua_ref, wda_ref),
                                           (pg / (1.0 + jnp.exp(va - vb)), wgb_ref, wub_ref, wdb_ref)):
            hg = _dot(xn, wg_ref[...])
            he = jnp.where(inside, wt * (hg * _sigmoid(hg) * _dot(xn, wu_ref[...])), 0.0)
            acc = acc + _dot(he.astype(BF16), wd_ref[...])
        ob[p][...] = acc

    interior = first & has_next & has_prev
    on_parity(interior, functools.partial(compute, True))

    @pl.when(first & has_next & ~has_prev)
    def _():
        for k in range(tmg):
            gather_start(snext_ref, k, 1)

    @pl.when(first & ~has_next & has_prev)
    def _():
        for k in range(tmg):
            scatter_start(sprev_ref, k, 1 - p_last)

    on_parity((first & ~interior) | (~first & (hi > lo)), functools.partial(compute, False))

    def end_tile(p):
        y = xb[p][...] + ob[p][...]
        ob[p][...] = _rmsnorm(y, gf_ref[...]) if final_norm else y

    on_parity(last, end_tile)

    @pl.when(v == pl.num_programs(0) - 1)
    def _():
        for k in range(tmg):
            scatter_start(scur_ref, k, p_last)
        if n_tiles > 1:
            scatter_wait(1 - p_last)
        scatter_wait(p_last)


def _moe_plan(bucket, n, tmg):
    i32 = jnp.int32
    src = jnp.argsort(bucket, stable=True).astype(i32)
    cnt = jnp.sum((bucket[:, None] == jnp.arange(N_BUCKETS, dtype=i32)[None, :]).astype(i32), axis=0)
    ends = jnp.cumsum(cnt)
    offs = ends - cnt
    nt = n // tmg
    cuts = jnp.sort(jnp.concatenate([jnp.arange(nt, dtype=i32) * tmg, offs[1:]]))
    nxt = jnp.concatenate([cuts[1:], jnp.full((1,), n, i32)])
    tile = jnp.minimum(cuts // tmg, nt - 1)
    bkt = jnp.minimum(jnp.sum((ends[None, :] <= cuts[:, None]).astype(i32), axis=1), N_BUCKETS - 1)
    change = (tile[1:] != tile[:-1]).astype(i32)
    one = jnp.ones((1,), i32)
    flags = jnp.concatenate([one, change]) + 2 * jnp.concatenate([change, one])
    grp = bkt // PAIRS_PER_GROUP
    pair = bkt % PAIRS_PER_GROUP
    ea = grp * EXP_PER_GROUP + jnp.take(jnp.array([0, 0, 0, 1, 1, 2], i32), pair)
    eb = grp * EXP_PER_GROUP + jnp.take(jnp.array([1, 2, 3, 2, 3, 3], i32), pair)
    return src, (tile, ea, eb, grp, cuts - tile * tmg, nxt - tile * tmg, flags)


def _moe_sorted(x, bucket, w, g_final, *, tmg, final_norm):
    B, T, D = x.shape
    n = B * T
    nt = n // tmg
    assert n % tmg == 0
    src, tables = _moe_plan(bucket.reshape(n, ROUTER_LANES)[:, 0].astype(jnp.int32), n, tmg)
    src3 = src.reshape(nt, 1, tmg)
    idx = lambda d: pl.BlockSpec((None, 1, tmg), lambda v, tile, *_: (jnp.clip(tile[v] + d, 0, nt - 1), 0, 0),
                                 memory_space=pltpu.SMEM)
    const = lambda shape: pl.BlockSpec(shape, lambda v, *_: (0,) * len(shape), pipeline_mode=pl.Buffered(1))
    w_a = lambda r, c: pl.BlockSpec((None, r, c), lambda v, tile, ea, eb, *_: (ea[v], 0, 0))
    w_b = lambda r, c: pl.BlockSpec((None, r, c), lambda v, tile, ea, eb, *_: (eb[v], 0, 0))
    any_spec = pl.BlockSpec(memory_space=pl.ANY)
    y = pl.pallas_call(
        functools.partial(_moe_sorted_kernel, tmg=tmg, n_tiles=nt, final_norm=final_norm),
        grid_spec=pltpu.PrefetchScalarGridSpec(
            num_scalar_prefetch=len(tables),
            grid=(tables[0].shape[0],),
            in_specs=[idx(-1), idx(0), idx(1), any_spec, const((1, D)), const((D, ROUTER_LANES)),
                      const((1, ROUTER_LANES)),
                      w_a(D, D_EXPERT), w_a(D, D_EXPERT), w_a(D_EXPERT, D),
                      w_b(D, D_EXPERT), w_b(D, D_EXPERT), w_b(D_EXPERT, D), const((1, D))],
            out_specs=any_spec,
            scratch_shapes=[pltpu.VMEM((tmg, D), F32)] * 4
                           + [pltpu.SemaphoreType.DMA((2,)), pltpu.SemaphoreType.DMA((2,))]),
        out_shape=jax.ShapeDtypeStruct((n, D), F32),
        compiler_params=pltpu.CompilerParams(dimension_semantics=("arbitrary",),
                                             vmem_limit_bytes=VMEM_LIMIT_BYTES, has_side_effects=True),
        name="moe_sorted",
    )(*tables, src3, src3, src3, x.reshape(n, D), w['g_ffn'], w['wr_hi'], w['b_r'],
      w['w_eg'], w['w_eu'], w['w_ed'], w['w_eg'], w['w_eu'], w['w_ed'], g_final)
    return y.reshape(B, T, D)


def _layer_weights(l, g_mix, w_in, b_gate, w_pool, s_pool, g_gv, b_gv, w_s, b_s, w_gmlp_out,
                   w_dw, b_dw, g_cln, b_cln, w_conv_out, w_out, g_xa, g_mem, w_xq, w_xk, w_xv, w_xo,
                   g_ffn, w_rg, b_rg, w_re, b_re, w_eg, w_eu, w_ed):
    row = lambda a: a[l].reshape(1, -1)
    wp = jnp.zeros((POOL_WIDTH, D_MODEL), F32)
    gout = D_MODEL // len(POOL_WINDOWS)
    for g in range(len(POOL_WINDOWS)):
        wp = wp.at[g * POOL_GDIM:(g + 1) * POOL_GDIM, g * gout:(g + 1) * gout].set(w_pool[l, g])
    wr = jnp.zeros((D_MODEL, ROUTER_LANES), F32)
    wr = wr.at[:, :N_GROUPS].set(w_rg[l]).at[:, LANE_E0:LANE_E0 + N_EXPERTS].set(w_re[l])
    wr_hi = wr.astype(BF16)
    br = jnp.zeros((1, ROUTER_LANES), F32)
    br = br.at[0, :N_GROUPS].set(b_rg[l]).at[0, LANE_E0:LANE_E0 + N_EXPERTS].set(b_re[l])
    return {
        'g_mix': row(g_mix), 'w_in': w_in[l].astype(BF16), 'b_gate': b_gate[l],
        'w_pool_bd': wp.astype(BF16), 's_pool': row(s_pool), 'g_gv': row(g_gv), 'b_gv': row(b_gv),
        'w_s': w_s[l], 'b_s_t': jnp.transpose(b_s[l]), 'w_gmlp_out': w_gmlp_out[l].astype(BF16),
        'w_dw': w_dw[l], 'b_dw': row(b_dw), 'g_cln': row(g_cln), 'b_cln': row(b_cln),
        'w_conv_out': w_conv_out[l].astype(BF16), 'w_out': w_out[l].astype(BF16),
        'g_xa': row(g_xa), 'g_mem': row(g_mem), 'w_xq': w_xq[l].astype(BF16),
        'w_xk': w_xk[l].astype(BF16), 'w_xv': w_xv[l].astype(BF16), 'w_xo': w_xo[l].astype(BF16),
        'g_ffn': row(g_ffn), 'wr_hi': wr_hi, 'wr_lo': (wr - wr_hi.astype(F32)).astype(BF16), 'b_r': br,
        'w_eg': w_eg[l].astype(BF16), 'w_eu': w_eu[l].astype(BF16), 'w_ed': w_ed[l].astype(BF16),
    }


def _attn_moe(x, k, v, w, g_final, *, tm, final_norm):
    n = x.shape[0] * x.shape[1]
    if n >= MOE_SORTED_MIN_ROWS:
        x, bucket = _cross_attn(x, k, v, w, tm=tm, route=True)
        return _moe_sorted(x, bucket, w, g_final, tmg=MOE_SORTED_TILE, final_norm=final_norm)
    x = _cross_attn(x, k, v, w, tm=tm, route=False)
    return _moe_dense(x, w, g_final, tm=_token_tile(n), final_norm=final_norm)


def _pad_hist(h, rows):
    return jnp.pad(h, ((0, 0), (rows - h.shape[1], 0), (0, 0)))


def _token_tile(t):
    return min(512, t)


def kernel(x_prompt, x_sample, mem_prompt, cache_pool, cache_conv, cache_mem_k, cache_mem_v, g_mix, w_in, b_gate, w_pool, s_pool, g_gv, b_gv, w_s, b_s, w_gmlp_out, w_dw, b_dw, g_cln, b_cln, w_conv_out, w_out, g_xa, g_mem, w_xq, w_xk, w_xv, w_xo, g_ffn, w_rg, b_rg, w_re, b_re, w_eg, w_eu, w_ed, g_final):
    depth = w_in.shape[0]
    bp, tp, _ = x_prompt.shape
    bs, ts, _ = x_sample.shape
    tmp, tms = _token_tile(tp), _token_tile(ts)
    gf = g_final.reshape(1, -1)
    xp, xs = x_prompt, x_sample
    pool0 = jnp.zeros((bp, POOL_HIST_PAD, POOL_WIDTH), F32)
    conv0 = jnp.zeros((bp, CONV_HIST_PAD, CONV_WIDTH), F32)
    outs = {k: [] for k in ('pp', 'pc', 'pk', 'pv', 'sp', 'sc', 'sv')}
    for l in range(depth):
        w = _layer_weights(l, g_mix, w_in, b_gate, w_pool, s_pool, g_gv, b_gv, w_s, b_s, w_gmlp_out,
                           w_dw, b_dw, g_cln, b_cln, w_conv_out, w_out, g_xa, g_mem, w_xq, w_xk, w_xv,
                           w_xo, g_ffn, w_rg, b_rg, w_re, b_re, w_eg, w_eu, w_ed)
        last = l == depth - 1
        xp, ph, ch = _mixer(xp, pool0, conv0, w, tm=tmp, pos0=0, emit_vn=False)
        mk, mv = _memory_kv(mem_prompt, w, tm=_token_tile(bp * MEM_LEN))
        xp = _attn_moe(xp, mk, mv, w, gf, tm=tmp, final_norm=last)
        outs['pp'].append(ph[:, POOL_HIST_PAD - POOL_HIST:])
        outs['pc'].append(ch[:, CONV_HIST_PAD - CONV_HIST:])
        outs['pk'].append(mk.reshape(bp, MEM_LEN, XA_HEADS, XA_HDIM))
        outs['pv'].append(mv.reshape(bp, MEM_LEN, XA_HEADS, XA_HDIM))
        xs, sh, sc, vn = _mixer(xs, _pad_hist(cache_pool[l], POOL_HIST_PAD),
                                _pad_hist(cache_conv[l], CONV_HIST_PAD), w,
                                tm=tms, pos0=PAST_LEN, emit_vn=True)
        xs = _attn_moe(xs, cache_mem_k[l].reshape(bs, MEM_LEN, D_MODEL),
                       cache_mem_v[l].reshape(bs, MEM_LEN, D_MODEL), w, gf, tm=tms, final_norm=last)
        outs['sp'].append(sh[:, POOL_HIST_PAD - POOL_HIST:])
        outs['sc'].append(sc[:, CONV_HIST_PAD - CONV_HIST:])
        outs['sv'].append(vn)
    st = lambda k: jnp.stack(outs[k], axis=0)
    return (xp, xs, st('pp'), st('pc'), st('pk'), st('pv'), st('sp'), st('sc'), st('sv'))
```

```python
import functools

import jax
import jax.numpy as jnp
from jax import lax
from jax.experimental import pallas as pl
from jax.experimental.pallas import tpu as pltpu

F32 = jnp.float32
BF16 = jnp.bfloat16

EPS = 1e-6
D_MODEL = 1024
PAST_LEN = 4096
CHUNK = 64
POOL_WIDTH = 256
POOL_WINDOWS = (2, 4, 8, 16)
POOL_GDIM = 64
POOL_HIST = 15
POOL_HIST_PAD = 32
GMLP_WIDTH = 512
GMLP_HEADS = 4
GMLP_HDIM = 128
GMLP_CHUNK = 128
CONV_WIDTH = 256
CONV_K = 31
CONV_HIST = 30
CONV_HIST_PAD = 32
SUBLANES = 8
MIXER_ROW_BLOCK = 512
CONV_ROWS = 64
N_BRANCH = 3
O_POOL = 0
O_GMLP = POOL_WIDTH
O_CONV = O_GMLP + 2 * GMLP_WIDTH
O_GATE = O_CONV + 2 * CONV_WIDTH
IN_COLS = O_GATE + N_BRANCH * D_MODEL
MEM_LEN = 256
XA_HEADS = 4
XA_HDIM = 256
N_GROUPS = 4
EXP_PER_GROUP = 4
N_EXPERTS = 16
D_EXPERT = 256
ROUTER_LANES = 128
LANE_E0 = N_GROUPS
PAIRS_PER_GROUP = 6
N_BUCKETS = N_GROUPS * PAIRS_PER_GROUP
MOE_SORTED_TILE = 256
MOE_SORTED_MIN_ROWS = 1024

VMEM_LIMIT_BYTES = 56 * 1024 * 1024


def _dot(a, b):
    return jnp.dot(a, b, preferred_element_type=F32)


def _rmsnorm(x, g):
    return x * lax.rsqrt(jnp.mean(x * x, axis=-1, keepdims=True) + EPS) * g


def _layernorm(x, g, b):
    mu = jnp.mean(x, axis=-1, keepdims=True)
    xc = x - mu
    var = jnp.mean(xc * xc, axis=-1, keepdims=True)
    return xc * lax.rsqrt(var + EPS) * g + b


def _sigmoid(x):
    return 1.0 / (1.0 + jnp.exp(-x))


def _const_spec(shape):
    nd = len(shape)
    return pl.BlockSpec(shape, lambda *_: (0,) * nd, pipeline_mode=pl.Buffered(1))


def _mixer_kernel(x_ref, ph_ref, ch_ref, g_ref, win_ref, bg_ref, wp_ref, sp_ref, ggv_ref, bgv_ref,
                  ws_ref, bs_ref, wgo_ref, wdw_ref, bdw_ref, gcl_ref, bcl_ref, wco_ref, wo_ref,
                  *rest, tm, pos0, emit_vn):
    if emit_vn:
        out_ref, pt_ref, at_ref, vn_ref, pbuf, abuf, sbuf, s2buf, s4buf, s8buf = rest
    else:
        out_ref, pt_ref, at_ref, pbuf, abuf, sbuf, s2buf, s4buf, s8buf = rest
        vn_ref = None
    t = pl.program_id(1)

    @pl.when(t == 0)
    def _():
        pbuf[0:POOL_HIST_PAD, :] = ph_ref[...]
        abuf[0:CONV_HIST_PAD, :] = ch_ref[...]

    rb = min(MIXER_ROW_BLOCK, tm)
    cr = min(GMLP_CHUNK, tm)
    bi = lax.broadcasted_iota(jnp.int32, (cr, cr), 0) // CHUNK
    bj = lax.broadcasted_iota(jnp.int32, (cr, cr), 1) // CHUNK
    ws = [jnp.where(bi >= bj, ws_ref[h, 0:cr, 0:cr], 0.0).astype(BF16) for h in range(GMLP_HEADS)]
    lane = lax.broadcasted_iota(jnp.int32, (rb, 128), 1)

    for rs in range(0, tm, rb):
        x = x_ref[rs:rs + rb, :]
        xn = _rmsnorm(x, g_ref[...]).astype(BF16)

        p = _dot(xn, win_ref[:, O_POOL:O_POOL + POOL_WIDTH])
        pbuf[POOL_HIST_PAD + rs:POOL_HIST_PAD + rs + rb, :] = p
        pos = (pos0 + t * tm + rs + lax.broadcasted_iota(jnp.int32, (rb, 1), 0) + 1).astype(F32)
        lo2, hi = rs + SUBLANES, POOL_HIST_PAD + rs + rb
        s2buf[lo2:hi, :] = pbuf[lo2:hi, :] + pbuf[lo2 - 1:hi - 1, :]
        s4buf[lo2 + 8:hi, :] = s2buf[lo2 + 8:hi, :] + s2buf[lo2 + 6:hi - 2, :]
        s8buf[lo2 + 16:hi, :] = s4buf[lo2 + 16:hi, 128:256] + s4buf[lo2 + 12:hi - 4, 128:256]
        r0 = POOL_HIST_PAD + rs
        s8 = s8buf[r0:hi, :]
        s16 = s8 + s8buf[r0 - 8:hi - 8, :]
        w2, w4, w8, w16 = (jnp.minimum(float(w), pos) for w in POOL_WINDOWS)
        means = [jnp.where(lane < POOL_GDIM, s2buf[r0:hi, 0:128] / w2, s4buf[r0:hi, 0:128] / w4),
                 jnp.where(lane < POOL_GDIM, s8 / w8, s16 / w16)]
        d = (jnp.concatenate(means, axis=1) - p).astype(BF16)
        ya = _dot(d, wp_ref[...]) * sp_ref[...]
        merged = _sigmoid(_dot(xn, win_ref[:, O_GATE:O_GATE + D_MODEL]) + bg_ref[0:1, :]) * ya

        z = jax.nn.gelu(_dot(xn, win_ref[:, O_GMLP:O_GMLP + 2 * GMLP_WIDTH]))
        u = z[:, :GMLP_WIDTH]
        vn = _layernorm(z[:, GMLP_WIDTH:], ggv_ref[...], bgv_ref[...])
        if emit_vn:
            vn_ref[rs:rs + rb, :] = vn
        vb = vn.astype(BF16)
        chunks = [jnp.concatenate(
            [_dot(ws[h], vb[c0:c0 + cr, h * GMLP_HDIM:(h + 1) * GMLP_HDIM]) + bs_ref[0:cr, h:h + 1]
             for h in range(GMLP_HEADS)], axis=1) for c0 in range(0, rb, cr)]
        mixed = chunks[0] if len(chunks) == 1 else jnp.concatenate(chunks, axis=0)
        yb = _dot((u * mixed).astype(BF16), wgo_ref[...])
        merged = merged + _sigmoid(_dot(xn, win_ref[:, O_GATE + D_MODEL:O_GATE + 2 * D_MODEL])
                                   + bg_ref[1:2, :]) * yb

        cc = _dot(xn, win_ref[:, O_CONV:O_CONV + 2 * CONV_WIDTH])
        a = cc[:, :CONV_WIDTH] * _sigmoid(cc[:, CONV_WIDTH:])
        abuf[CONV_HIST_PAD + rs:CONV_HIST_PAD + rs + rb, :] = a
        c_lo = 0 if rs == 0 else rs + CONV_HIST_PAD - SUBLANES
        c_hi = rs + rb + CONV_HIST_PAD - SUBLANES
        for r in range(1, SUBLANES):
            sbuf[r - 1, c_lo:c_hi, :] = abuf[c_lo + r:c_hi + r, :]

        def tap(k, r0, n):
            q, r = divmod(CONV_HIST_PAD - CONV_HIST + k, SUBLANES)
            lo = SUBLANES * q + r0
            return abuf[lo:lo + n, :] if r == 0 else sbuf[r - 1, lo:lo + n, :]

        rc = min(CONV_ROWS, rb)
        hs = []
        for r0 in range(rs, rs + rb, rc):
            acc = tap(0, r0, rc) * wdw_ref[0:1, :]
            for k in range(1, CONV_K):
                acc = acc + tap(k, r0, rc) * wdw_ref[k:k + 1, :]
            hln = _layernorm(acc + bdw_ref[...], gcl_ref[...], bcl_ref[...])
            hs.append((hln * _sigmoid(hln)).astype(BF16))
        yc = _dot(hs[0] if len(hs) == 1 else jnp.concatenate(hs, axis=0), wco_ref[...])
        merged = merged + _sigmoid(_dot(xn, win_ref[:, O_GATE + 2 * D_MODEL:O_GATE + 3 * D_MODEL])
                                   + bg_ref[2:3, :]) * yc

        out_ref[rs:rs + rb, :] = x + _dot(merged.astype(BF16), wo_ref[...])

    p_tail = pbuf[tm:tm + POOL_HIST_PAD, :]
    a_tail = abuf[tm:tm + CONV_HIST_PAD, :]
    pbuf[0:POOL_HIST_PAD, :] = p_tail
    abuf[0:CONV_HIST_PAD, :] = a_tail
    pt_ref[...] = p_tail
    at_ref[...] = a_tail


def _mixer(x, pool_hist, conv_hist, w, *, tm, pos0, emit_vn):
    B, T, D = x.shape
    assert T % tm == 0 and tm >= CONV_HIST_PAD and (tm % GMLP_CHUNK == 0 or tm == T)
    tok = pl.BlockSpec((None, tm, D), lambda b, t: (b, t, 0))
    per_b = lambda r, c: pl.BlockSpec((None, r, c), lambda b, t: (b, 0, 0))
    weights = [w['g_mix'], w['w_in'], w['b_gate'], w['w_pool_bd'], w['s_pool'], w['g_gv'], w['b_gv'],
               w['w_s'], w['b_s_t'], w['w_gmlp_out'], w['w_dw'], w['b_dw'], w['g_cln'], w['b_cln'],
               w['w_conv_out'], w['w_out']]
    out_shape = [jax.ShapeDtypeStruct((B, T, D), F32),
                 jax.ShapeDtypeStruct((B, POOL_HIST_PAD, POOL_WIDTH), F32),
                 jax.ShapeDtypeStruct((B, CONV_HIST_PAD, CONV_WIDTH), F32)]
    out_specs = [tok, per_b(POOL_HIST_PAD, POOL_WIDTH), per_b(CONV_HIST_PAD, CONV_WIDTH)]
    if emit_vn:
        out_shape.append(jax.ShapeDtypeStruct((B, T, GMLP_WIDTH), F32))
        out_specs.append(pl.BlockSpec((None, tm, GMLP_WIDTH), lambda b, t: (b, t, 0)))
    return pl.pallas_call(
        functools.partial(_mixer_kernel, tm=tm, pos0=pos0, emit_vn=emit_vn),
        grid=(B, T // tm),
        in_specs=[tok, per_b(POOL_HIST_PAD, POOL_WIDTH), per_b(CONV_HIST_PAD, CONV_WIDTH)]
                 + [_const_spec(a.shape) for a in weights],
        out_specs=out_specs,
        out_shape=out_shape,
        scratch_shapes=[pltpu.VMEM((POOL_HIST_PAD + tm, POOL_WIDTH), F32),
                        pltpu.VMEM((CONV_HIST_PAD + tm, CONV_WIDTH), F32),
                        pltpu.VMEM((SUBLANES - 1, CONV_HIST_PAD + tm - SUBLANES, CONV_WIDTH), F32),
                        pltpu.VMEM((POOL_HIST_PAD + tm, POOL_WIDTH), F32),
                        pltpu.VMEM((POOL_HIST_PAD + tm, POOL_WIDTH), F32),
                        pltpu.VMEM((POOL_HIST_PAD + tm, POOL_WIDTH // 2), F32)],
        compiler_params=pltpu.CompilerParams(dimension_semantics=("arbitrary", "arbitrary"),
                                             vmem_limit_bytes=VMEM_LIMIT_BYTES),
        name="mixer",
    )(x, pool_hist, conv_hist, *weights)


def _kv_kernel(m_ref, g_ref, wk_ref, wv_ref, k_ref, v_ref):
    mn = _rmsnorm(m_ref[...], g_ref[...]).astype(BF16)
    k_ref[...] = _dot(mn, wk_ref[...])
    v_ref[...] = _dot(mn, wv_ref[...])


def _memory_kv(mem, w, *, tm):
    B, M, D = mem.shape
    rows = mem.reshape(B * M, D)
    blk = pl.BlockSpec((tm, D), lambda i: (i, 0))
    k, v = pl.pallas_call(
        _kv_kernel,
        grid=(B * M // tm,),
        in_specs=[blk, _const_spec((1, D)), _const_spec((D, D)), _const_spec((D, D))],
        out_specs=[blk, blk],
        out_shape=[jax.ShapeDtypeStruct((B * M, D), F32)] * 2,
        compiler_params=pltpu.CompilerParams(dimension_semantics=("arbitrary",),
                                             vmem_limit_bytes=VMEM_LIMIT_BYTES),
        name="memory_kv",
    )(rows, w['g_mem'], w['w_xk'], w['w_xv'])
    return k.reshape(B, M, D), v.reshape(B, M, D)


def _xattn_kernel(x_ref, k_ref, v_ref, g_ref, wq_ref, wo_ref, *rest, route):
    if route:
        gf_ref, wrc_ref, br_ref, out_ref, bkt_ref, kb, vb = rest
    else:
        out_ref, kb, vb = rest

    @pl.when(pl.program_id(1) == 0)
    def _():
        kb[...] = k_ref[...].astype(BF16)
        vb[...] = v_ref[...].astype(BF16)

    x = x_ref[...]
    xn = _rmsnorm(x, g_ref[...]).astype(BF16)
    q = (_dot(xn, wq_ref[...]) * (XA_HDIM ** -0.5)).astype(BF16)
    heads = []
    for h in range(XA_HEADS):
        sl = slice(h * XA_HDIM, (h + 1) * XA_HDIM)
        s = lax.dot_general(q[:, sl], kb[:, sl], (((1,), (1,)), ((), ())), preferred_element_type=F32)
        e = jnp.exp(s - jnp.max(s, axis=-1, keepdims=True))
        pr = (e / jnp.sum(e, axis=-1, keepdims=True)).astype(BF16)
        heads.append(_dot(pr, vb[:, sl]).astype(BF16))
    o = jnp.concatenate(heads, axis=1)
    y = x + _dot(o, wo_ref[...])
    out_ref[...] = y
    if route:
        lane, gsel, _, i1, i2, _, _ = _route_select(_rmsnorm(y, gf_ref[...]), wrc_ref, br_ref)
        e_lo = LANE_E0 + gsel * EXP_PER_GROUP
        a = jnp.minimum(i1, i2) - e_lo
        b = jnp.maximum(i1, i2) - e_lo
        pair = jnp.where(a == 0, 0, jnp.where(a == 1, 3, 5)) + (b - a - 1)
        bkt_ref[...] = jnp.broadcast_to((gsel * PAIRS_PER_GROUP + pair).astype(F32), bkt_ref.shape)


def _cross_attn(x, k, v, w, *, tm, route):
    B, T, D = x.shape
    tok = pl.BlockSpec((None, tm, D), lambda b, t: (b, t, 0))
    mem = pl.BlockSpec((None, MEM_LEN, D), lambda b, t: (b, 0, 0))
    in_specs = [tok, mem, mem, _const_spec((1, D)), _const_spec((D, D)), _const_spec((D, D))]
    args = [x, k, v, w['g_xa'], w['w_xq'], w['w_xo']]
    out_specs = [tok]
    out_shape = [jax.ShapeDtypeStruct((B, T, D), F32)]
    if route:
        in_specs += [_const_spec((1, D)), _const_spec((D, 2 * ROUTER_LANES)), _const_spec((1, ROUTER_LANES))]
        args += [w['g_ffn'], w['wr_cat'], w['b_r']]
        out_specs.append(pl.BlockSpec((None, tm, ROUTER_LANES), lambda b, t: (b, t, 0)))
        out_shape.append(jax.ShapeDtypeStruct((B, T, ROUTER_LANES), F32))
    res = pl.pallas_call(
        functools.partial(_xattn_kernel, route=route),
        grid=(B, T // tm),
        in_specs=in_specs,
        out_specs=out_specs,
        out_shape=out_shape,
        scratch_shapes=[pltpu.VMEM((MEM_LEN, D), BF16), pltpu.VMEM((MEM_LEN, D), BF16)],
        compiler_params=pltpu.CompilerParams(dimension_semantics=("arbitrary", "arbitrary"),
                                             vmem_limit_bytes=VMEM_LIMIT_BYTES),
        name="cross_attn_route" if route else "cross_attn",
    )(*args)
    return res if route else res[0]


def _route_select(xn, wr_cat_ref, br_ref):
    x_hi = xn.astype(BF16)
    x_lo = (xn - x_hi.astype(F32)).astype(BF16)
    hh_hl = _dot(x_hi, wr_cat_ref[...])
    logits = (hh_hl[:, :ROUTER_LANES] + hh_hl[:, ROUTER_LANES:]
              + _dot(x_lo, wr_cat_ref[:, :ROUTER_LANES]) + br_ref[...])
    lane = lax.broadcasted_iota(jnp.int32, logits.shape, 1)
    neg = jnp.float32(-jnp.inf)
    big = jnp.int32(ROUTER_LANES)

    def first_argmax(vals):
        m = jnp.max(vals, axis=-1, keepdims=True)
        return m, jnp.min(jnp.where(vals == m, lane, big), axis=-1, keepdims=True)

    gl = jnp.where(lane < N_GROUPS, logits, neg)
    gmax, gsel = first_argmax(gl)
    pg = 1.0 / jnp.sum(jnp.exp(gl - gmax), axis=-1, keepdims=True)
    e_lo = LANE_E0 + gsel * EXP_PER_GROUP
    el = jnp.where((lane >= e_lo) & (lane < e_lo + EXP_PER_GROUP), logits, neg)
    v1, i1 = first_argmax(el)
    v2, i2 = first_argmax(jnp.where(lane == i1, neg, el))
    return lane, gsel, pg, i1, i2, v1, v2


def _route(xn, wr_cat_ref, br_ref):
    lane, _, pg, i1, i2, v1, v2 = _route_select(xn, wr_cat_ref, br_ref)
    r = jnp.exp(v2 - v1)
    pe1 = pg / (1.0 + r)
    pe2 = pg * r / (1.0 + r)
    return jnp.where(lane == i1, pe1, 0.0) + jnp.where(lane == i2, pe2, 0.0)


def _moe_dense_kernel(x_ref, g_ref, wrc_ref, br_ref, wgu_ref, wd_ref, gf_ref, out_ref, he_buf,
                      *, final_norm):
    x = x_ref[...]
    xn = _rmsnorm(x, g_ref[...])
    comb = _route(xn, wrc_ref, br_ref)
    xb = xn.astype(BF16)
    lane = lax.broadcasted_iota(jnp.int32, comb.shape, 1)
    for e in range(N_EXPERTS):
        c = jnp.sum(jnp.where(lane == e + LANE_E0, comb, 0.0), axis=-1, keepdims=True)
        h = _dot(xb, wgu_ref[e])
        hg = h[:, :D_EXPERT]
        he = c * (hg * _sigmoid(hg) * h[:, D_EXPERT:])
        he_buf[:, e * D_EXPERT:(e + 1) * D_EXPERT] = he.astype(BF16)
    y = x + _dot(he_buf[...], wd_ref[...])
    out_ref[...] = _rmsnorm(y, gf_ref[...]) if final_norm else y


def _moe_dense(x, w, g_final, *, tm, final_norm):
    B, T, D = x.shape
    rows = x.reshape(B * T, D)
    tok = pl.BlockSpec((tm, D), lambda i: (i, 0))
    out = pl.pallas_call(
        functools.partial(_moe_dense_kernel, final_norm=final_norm),
        grid=(B * T // tm,),
        in_specs=[tok, _const_spec((1, D)), _const_spec((D, 2 * ROUTER_LANES)),
                  _const_spec((1, ROUTER_LANES)), _const_spec((N_EXPERTS, D, 2 * D_EXPERT)),
                  _const_spec((N_EXPERTS * D_EXPERT, D)), _const_spec((1, D))],
        out_specs=tok,
        out_shape=jax.ShapeDtypeStruct((B * T, D), F32),
        scratch_shapes=[pltpu.VMEM((tm, N_EXPERTS * D_EXPERT), BF16)],
        compiler_params=pltpu.CompilerParams(dimension_semantics=("arbitrary",),
                                             vmem_limit_bytes=VMEM_LIMIT_BYTES),
        name="moe_dense",
    )(rows, w['g_ffn'], w['wr_cat'], w['b_r'], jnp.concatenate([w['w_eg'], w['w_eu']], axis=-1),
      w['w_ed'].reshape(N_EXPERTS * D_EXPERT, D), g_final)
    return out.reshape(B, T, D)


def _moe_sorted_kernel(tile_ref, ea_ref, eb_ref, grp_ref, lo_ref, hi_ref, flag_ref,
                       sprev_ref, scur_ref, snext_ref, x_hbm, g_ref, wrh_ref, br_ref,
                       wga_ref, wua_ref, wda_ref, wgb_ref, wub_ref, wdb_ref, gf_ref,
                       y_hbm, xb0, xb1, ob0, ob1, gsem, ssem, *, tmg, n_tiles, final_norm):
    v = pl.program_id(0)
    tile, lo, hi, flags = tile_ref[v], lo_ref[v], hi_ref[v], flag_ref[v]
    xb, ob = (xb0, xb1), (ob0, ob1)
    first = (flags & 1) == 1
    last = (flags & 2) == 2
    has_next = tile < n_tiles - 1
    has_prev = tile > 0
    p_last = (n_tiles - 1) % 2

    def gather_start(idx_ref, k, p):
        pltpu.make_async_copy(x_hbm.at[pl.ds(idx_ref[0, k], 1)], xb[p].at[pl.ds(k, 1)], gsem.at[p]).start()

    def scatter_start(idx_ref, k, p):
        pltpu.make_async_copy(ob[p].at[pl.ds(k, 1)], y_hbm.at[pl.ds(idx_ref[0, k], 1)], ssem.at[p]).start()

    def gather_wait(p):
        pltpu.make_async_copy(x_hbm.at[pl.ds(0, tmg)], xb[p], gsem.at[p]).wait()

    def scatter_wait(p):
        pltpu.make_async_copy(ob[p], y_hbm.at[pl.ds(0, tmg)], ssem.at[p]).wait()

    def on_parity(cond, fn):
        for p in (0, 1):
            pl.when(cond & (tile % 2 == p))(functools.partial(fn, p))

    @pl.when(v == 0)
    def _():
        for k in range(tmg):
            gather_start(scur_ref, k, 0)

    def begin_tile(p):
        gather_wait(p)

        @pl.when(tile >= 2)
        def _():
            scatter_wait(p)

        ob[p][...] = jnp.zeros((tmg, D_MODEL), F32)

    on_parity(first, begin_tile)

    def compute(overlap_dma, p):
        if overlap_dma:
            for k in range(tmg):
                gather_start(snext_ref, k, 1 - p)
            for k in range(tmg):
                scatter_start(sprev_ref, k, 1 - p)
        xn = _rmsnorm(xb[p][...], g_ref[...]).astype(BF16)
        logits = _dot(xn, wrh_ref[...]) + br_ref[...]
        lane = lax.broadcasted_iota(jnp.int32, logits.shape, 1)
        pick = lambda l: jnp.sum(jnp.where(lane == l, logits, 0.0), axis=-1, keepdims=True)
        gl = jnp.where(lane < N_GROUPS, logits, jnp.float32(-jnp.inf))
        gmax = jnp.max(gl, axis=-1, keepdims=True)
        pg = jnp.exp(pick(grp_ref[v]) - gmax) / jnp.sum(jnp.exp(gl - gmax), axis=-1, keepdims=True)
        va, vb = pick(LANE_E0 + ea_ref[v]), pick(LANE_E0 + eb_ref[v])
        row = lax.broadcasted_iota(jnp.int32, (tmg, 1), 0)
        inside = (row >= lo) & (row < hi)
        acc = ob[p][...]
        for wt, wg_ref, wu_ref, wd_ref in ((pg / (1.0 + jnp.exp(vb - va)), wga_ref, wua_ref, wda_ref),
                                           (pg / (1.0 + jnp.exp(va - vb)), wgb_ref, wub_ref, wdb_ref)):
            hg = _dot(xn, wg_ref[...])
            he = jnp.where(inside, wt * (hg * _sigmoid(hg) * _dot(xn, wu_ref[...])), 0.0)
            acc = acc + _dot(he.astype(BF16), wd_ref[...])
        ob[p][...] = acc

    interior = first & has_next & has_prev
    on_parity(interior, functools.partial(compute, True))

    @pl.when(first & has_next & ~has_prev)
    def _():
        for k in range(tmg):
            gather_start(snext_ref, k, 1)

    @pl.when(first & ~has_next & has_prev)
    def _():
        for k in range(tmg):
            scatter_start(sprev_ref, k, 1 - p_last)

    on_parity((first & ~interior) | (~first & (hi > lo)), functools.partial(compute, False))

    def end_tile(p):
        y = xb[p][...] + ob[p][...]
        ob[p][...] = _rmsnorm(y, gf_ref[...]) if final_norm else y

    on_parity(last, end_tile)

    @pl.when(v == pl.num_programs(0) - 1)
    def _():
        for k in range(tmg):
            scatter_start(scur_ref, k, p_last)
        if n_tiles > 1:
            scatter_wait(1 - p_last)
        scatter_wait(p_last)


def _moe_plan(bucket, n, tmg):
    i32 = jnp.int32
    src = jnp.argsort(bucket, stable=True).astype(i32)
    cnt = jnp.sum((bucket[:, None] == jnp.arange(N_BUCKETS, dtype=i32)[None, :]).astype(i32), axis=0)
    ends = jnp.cumsum(cnt)
    offs = ends - cnt
    nt = n // tmg
    cuts = jnp.sort(jnp.concatenate([jnp.arange(nt, dtype=i32) * tmg, offs[1:]]))
    nxt = jnp.concatenate([cuts[1:], jnp.full((1,), n, i32)])
    tile = jnp.minimum(cuts // tmg, nt - 1)
    bkt = jnp.minimum(jnp.sum((ends[None, :] <= cuts[:, None]).astype(i32), axis=1), N_BUCKETS - 1)
    change = (tile[1:] != tile[:-1]).astype(i32)
    one = jnp.ones((1,), i32)
    flags = jnp.concatenate([one, change]) + 2 * jnp.concatenate([change, one])
    grp = bkt // PAIRS_PER_GROUP
    pair = bkt % PAIRS_PER_GROUP
    ea = grp * EXP_PER_GROUP + jnp.take(jnp.array([0, 0, 0, 1, 1, 2], i32), pair)
    eb = grp * EXP_PER_GROUP + jnp.take(jnp.array([1, 2, 3, 2, 3, 3], i32), pair)
    return src, (tile, ea, eb, grp, cuts - tile * tmg, nxt - tile * tmg, flags)


def _moe_sorted(x, bucket, w, g_final, *, tmg, final_norm):
    B, T, D = x.shape
    n = B * T
    nt = n // tmg
    assert n % tmg == 0
    src, tables = _moe_plan(bucket.reshape(n, ROUTER_LANES)[:, 0].astype(jnp.int32), n, tmg)
    src3 = src.reshape(nt, 1, tmg)
    idx = lambda d: pl.BlockSpec((None, 1, tmg), lambda v, tile, *_: (jnp.clip(tile[v] + d, 0, nt - 1), 0, 0),
                                 memory_space=pltpu.SMEM)
    const = lambda shape: pl.BlockSpec(shape, lambda v, *_: (0,) * len(shape), pipeline_mode=pl.Buffered(1))
    w_a = lambda r, c: pl.BlockSpec((None, r, c), lambda v, tile, ea, eb, *_: (ea[v], 0, 0))
    w_b = lambda r, c: pl.BlockSpec((None, r, c), lambda v, tile, ea, eb, *_: (eb[v], 0, 0))
    any_spec = pl.BlockSpec(memory_space=pl.ANY)
    y = pl.pallas_call(
        functools.partial(_moe_sorted_kernel, tmg=tmg, n_tiles=nt, final_norm=final_norm),
        grid_spec=pltpu.PrefetchScalarGridSpec(
            num_scalar_prefetch=len(tables),
            grid=(tables[0].shape[0],),
            in_specs=[idx(-1), idx(0), idx(1), any_spec, const((1, D)), const((D, ROUTER_LANES)),
                      const((1, ROUTER_LANES)),
                      w_a(D, D_EXPERT), w_a(D, D_EXPERT), w_a(D_EXPERT, D),
                      w_b(D, D_EXPERT), w_b(D, D_EXPERT), w_b(D_EXPERT, D), const((1, D))],
            out_specs=any_spec,
            scratch_shapes=[pltpu.VMEM((tmg, D), F32)] * 4
                           + [pltpu.SemaphoreType.DMA((2,)), pltpu.SemaphoreType.DMA((2,))]),
        out_shape=jax.ShapeDtypeStruct((n, D), F32),
        compiler_params=pltpu.CompilerParams(dimension_semantics=("arbitrary",),
                                             vmem_limit_bytes=VMEM_LIMIT_BYTES, has_side_effects=True),
        name="moe_sorted",
    )(*tables, src3, src3, src3, x.reshape(n, D), w['g_ffn'], w['wr_hi'], w['b_r'],
      w['w_eg'], w['w_eu'], w['w_ed'], w['w_eg'], w['w_eu'], w['w_ed'], g_final)
    return y.reshape(B, T, D)


def _layer_weights(l, g_mix, w_in, b_gate, w_pool, s_pool, g_gv, b_gv, w_s, b_s, w_gmlp_out,
                   w_dw, b_dw, g_cln, b_cln, w_conv_out, w_out, g_xa, g_mem, w_xq, w_xk, w_xv, w_xo,
                   g_ffn, w_rg, b_rg, w_re, b_re, w_eg, w_eu, w_ed):
    row = lambda a: a[l].reshape(1, -1)
    wp = jnp.zeros((POOL_WIDTH, D_MODEL), F32)
    gout = D_MODEL // len(POOL_WINDOWS)
    for g in range(len(POOL_WINDOWS)):
        wp = wp.at[g * POOL_GDIM:(g + 1) * POOL_GDIM, g * gout:(g + 1) * gout].set(w_pool[l, g])
    wr = jnp.zeros((D_MODEL, ROUTER_LANES), F32)
    wr = wr.at[:, :N_GROUPS].set(w_rg[l]).at[:, LANE_E0:LANE_E0 + N_EXPERTS].set(w_re[l])
    wr_hi = wr.astype(BF16)
    br = jnp.zeros((1, ROUTER_LANES), F32)
    br = br.at[0, :N_GROUPS].set(b_rg[l]).at[0, LANE_E0:LANE_E0 + N_EXPERTS].set(b_re[l])
    return {
        'g_mix': row(g_mix), 'w_in': w_in[l].astype(BF16), 'b_gate': b_gate[l],
        'w_pool_bd': wp.astype(BF16), 's_pool': row(s_pool), 'g_gv': row(g_gv), 'b_gv': row(b_gv),
        'w_s': w_s[l], 'b_s_t': jnp.transpose(b_s[l]), 'w_gmlp_out': w_gmlp_out[l].astype(BF16),
        'w_dw': w_dw[l], 'b_dw': row(b_dw), 'g_cln': row(g_cln), 'b_cln': row(b_cln),
        'w_conv_out': w_conv_out[l].astype(BF16), 'w_out': w_out[l].astype(BF16),
        'g_xa': row(g_xa), 'g_mem': row(g_mem), 'w_xq': w_xq[l].astype(BF16),
        'w_xk': w_xk[l].astype(BF16), 'w_xv': w_xv[l].astype(BF16), 'w_xo': w_xo[l].astype(BF16),
        'g_ffn': row(g_ffn), 'wr_hi': wr_hi, 'b_r': br,
        'wr_cat': jnp.concatenate([wr_hi, (wr - wr_hi.astype(F32)).astype(BF16)], axis=1),
        'w_eg': w_eg[l].astype(BF16), 'w_eu': w_eu[l].astype(BF16), 'w_ed': w_ed[l].astype(BF16),
    }


def _attn_moe(x, k, v, w, g_final, *, tm, final_norm):
    n = x.shape[0] * x.shape[1]
    if n >= MOE_SORTED_MIN_ROWS:
        x, bucket = _cross_attn(x, k, v, w, tm=tm, route=True)
        return _moe_sorted(x, bucket, w, g_final, tmg=MOE_SORTED_TILE, final_norm=final_norm)
    x = _cross_attn(x, k, v, w, tm=tm, route=False)
    return _moe_dense(x, w, g_final, tm=_token_tile(n), final_norm=final_norm)


def _pad_hist(h, rows):
    return jnp.pad(h, ((0, 0), (rows - h.shape[1], 0), (0, 0)))


def _token_tile(t):
    return min(1024, t)


def kernel(x_prompt, x_sample, mem_prompt, cache_pool, cache_conv, cache_mem_k, cache_mem_v, g_mix, w_in, b_gate, w_pool, s_pool, g_gv, b_gv, w_s, b_s, w_gmlp_out, w_dw, b_dw, g_cln, b_cln, w_conv_out, w_out, g_xa, g_mem, w_xq, w_xk, w_xv, w_xo, g_ffn, w_rg, b_rg, w_re, b_re, w_eg, w_eu, w_ed, g_final):
    depth = w_in.shape[0]
    bp, tp, _ = x_prompt.shape
    bs, ts, _ = x_sample.shape
    tmp, tms = _token_tile(tp), _token_tile(ts)
    gf = g_final.reshape(1, -1)
    xp, xs = x_prompt, x_sample
    pool0 = jnp.zeros((bp, POOL_HIST_PAD, POOL_WIDTH), F32)
    conv0 = jnp.zeros((bp, CONV_HIST_PAD, CONV_WIDTH), F32)
    outs = {k: [] for k in ('pp', 'pc', 'pk', 'pv', 'sp', 'sc', 'sv')}
    for l in range(depth):
        w = _layer_weights(l, g_mix, w_in, b_gate, w_pool, s_pool, g_gv, b_gv, w_s, b_s, w_gmlp_out,
                           w_dw, b_dw, g_cln, b_cln, w_conv_out, w_out, g_xa, g_mem, w_xq, w_xk, w_xv,
                           w_xo, g_ffn, w_rg, b_rg, w_re, b_re, w_eg, w_eu, w_ed)
        last = l == depth - 1
        xp, ph, ch = _mixer(xp, pool0, conv0, w, tm=tmp, pos0=0, emit_vn=False)
        mk, mv = _memory_kv(mem_prompt, w, tm=_token_tile(bp * MEM_LEN))
        xp = _attn_moe(xp, mk, mv, w, gf, tm=tmp, final_norm=last)
        outs['pp'].append(ph[:, POOL_HIST_PAD - POOL_HIST:])
        outs['pc'].append(ch[:, CONV_HIST_PAD - CONV_HIST:])
        outs['pk'].append(mk.reshape(bp, MEM_LEN, XA_HEADS, XA_HDIM))
        outs['pv'].append(mv.reshape(bp, MEM_LEN, XA_HEADS, XA_HDIM))
        xs, sh, sc, vn = _mixer(xs, _pad_hist(cache_pool[l], POOL_HIST_PAD),
                                _pad_hist(cache_conv[l], CONV_HIST_PAD), w,
                                tm=tms, pos0=PAST_LEN, emit_vn=True)
        xs = _attn_moe(xs, cache_mem_k[l].reshape(bs, MEM_LEN, D_MODEL),
                       cache_mem_v[l].reshape(bs, MEM_LEN, D_MODEL), w, gf, tm=tms, final_norm=last)
        outs['sp'].append(sh[:, POOL_HIST_PAD - POOL_HIST:])
        outs['sc'].append(sc[:, CONV_HIST_PAD - CONV_HIST:])
        outs['sv'].append(vn)
    st = lambda k: jnp.stack(outs[k], axis=0)
    return (xp, xs, st('pp'), st('pc'), st('pk'), st('pv'), st('sp'), st('sc'), st('sv'))
```

```python
import functools

import jax
import jax.numpy as jnp
from jax import lax
from jax.experimental import pallas as pl
from jax.experimental.pallas import tpu as pltpu

F32 = jnp.float32
BF16 = jnp.bfloat16

EPS = 1e-6
D_MODEL = 1024
PAST_LEN = 4096
CHUNK = 64
POOL_WIDTH = 256
POOL_WINDOWS = (2, 4, 8, 16)
POOL_GDIM = 64
POOL_HIST = 15
POOL_HIST_PAD = 32
GMLP_WIDTH = 512
GMLP_HEADS = 4
GMLP_HDIM = 128
GMLP_CHUNK = 128
CONV_WIDTH = 256
CONV_K = 31
CONV_HIST = 30
CONV_HIST_PAD = 32
SUBLANES = 8
MIXER_ROW_BLOCK = 512
CONV_ROWS = 64
N_BRANCH = 3
O_POOL = 0
O_GMLP = POOL_WIDTH
O_CONV = O_GMLP + 2 * GMLP_WIDTH
O_GATE = O_CONV + 2 * CONV_WIDTH
IN_COLS = O_GATE + N_BRANCH * D_MODEL
MEM_LEN = 256
XA_HEADS = 4
XA_HDIM = 256
N_GROUPS = 4
EXP_PER_GROUP = 4
N_EXPERTS = 16
D_EXPERT = 256
ROUTER_LANES = 128
LANE_E0 = N_GROUPS
PAIRS_PER_GROUP = 6
N_BUCKETS = N_GROUPS * PAIRS_PER_GROUP
MOE_SORTED_TILE = 256
MOE_SORTED_MIN_ROWS = 1024

VMEM_LIMIT_BYTES = 56 * 1024 * 1024


def _dot(a, b):
    return jnp.dot(a, b, preferred_element_type=F32)


def _rmsnorm(x, g):
    return x * lax.rsqrt(jnp.mean(x * x, axis=-1, keepdims=True) + EPS) * g


def _layernorm(x, g, b):
    mu = jnp.mean(x, axis=-1, keepdims=True)
    xc = x - mu
    var = jnp.mean(xc * xc, axis=-1, keepdims=True)
    return xc * lax.rsqrt(var + EPS) * g + b


def _sigmoid(x):
    return 1.0 / (1.0 + jnp.exp(-x))


def _const_spec(shape):
    nd = len(shape)
    return pl.BlockSpec(shape, lambda *_: (0,) * nd, pipeline_mode=pl.Buffered(1))


def _mixer_kernel(x_ref, ph_ref, ch_ref, g_ref, win_ref, bg_ref, wp_ref, sp_ref, ggv_ref, bgv_ref,
                  ws_ref, bs_ref, wgo_ref, wdw_ref, bdw_ref, gcl_ref, bcl_ref, wco_ref, wo_ref,
                  *rest, tm, pos0, emit_vn):
    if emit_vn:
        out_ref, pt_ref, at_ref, vn_ref, pbuf, abuf, sbuf, s2buf, s4buf, s8buf = rest
    else:
        out_ref, pt_ref, at_ref, pbuf, abuf, sbuf, s2buf, s4buf, s8buf = rest
        vn_ref = None
    t = pl.program_id(1)

    @pl.when(t == 0)
    def _():
        pbuf[0:POOL_HIST_PAD, :] = ph_ref[...]
        abuf[0:CONV_HIST_PAD, :] = ch_ref[...]

    rb = min(MIXER_ROW_BLOCK, tm)
    cr = min(GMLP_CHUNK, tm)
    bi = lax.broadcasted_iota(jnp.int32, (cr, cr), 0) // CHUNK
    bj = lax.broadcasted_iota(jnp.int32, (cr, cr), 1) // CHUNK
    ws = [jnp.where(bi >= bj, ws_ref[h, 0:cr, 0:cr], 0.0).astype(BF16) for h in range(GMLP_HEADS)]
    lane = lax.broadcasted_iota(jnp.int32, (rb, 128), 1)

    for rs in range(0, tm, rb):
        x = x_ref[rs:rs + rb, :]
        xn = _rmsnorm(x, g_ref[...]).astype(BF16)

        p = _dot(xn, win_ref[:, O_POOL:O_POOL + POOL_WIDTH])
        pbuf[POOL_HIST_PAD + rs:POOL_HIST_PAD + rs + rb, :] = p
        pos = (pos0 + t * tm + rs + lax.broadcasted_iota(jnp.int32, (rb, 1), 0) + 1).astype(F32)
        lo2, hi = rs + SUBLANES, POOL_HIST_PAD + rs + rb
        s2buf[lo2:hi, :] = pbuf[lo2:hi, :] + pbuf[lo2 - 1:hi - 1, :]
        s4buf[lo2 + 8:hi, :] = s2buf[lo2 + 8:hi, :] + s2buf[lo2 + 6:hi - 2, :]
        s8buf[lo2 + 16:hi, :] = s4buf[lo2 + 16:hi, 128:256] + s4buf[lo2 + 12:hi - 4, 128:256]
        r0 = POOL_HIST_PAD + rs
        s8 = s8buf[r0:hi, :]
        s16 = s8 + s8buf[r0 - 8:hi - 8, :]
        w2, w4, w8, w16 = (jnp.minimum(float(w), pos) for w in POOL_WINDOWS)
        means = [jnp.where(lane < POOL_GDIM, s2buf[r0:hi, 0:128] / w2, s4buf[r0:hi, 0:128] / w4),
                 jnp.where(lane < POOL_GDIM, s8 / w8, s16 / w16)]
        d = (jnp.concatenate(means, axis=1) - p).astype(BF16)
        ya = _dot(d, wp_ref[...]) * sp_ref[...]
        merged = _sigmoid(_dot(xn, win_ref[:, O_GATE:O_GATE + D_MODEL]) + bg_ref[0:1, :]) * ya

        z = jax.nn.gelu(_dot(xn, win_ref[:, O_GMLP:O_GMLP + 2 * GMLP_WIDTH]))
        u = z[:, :GMLP_WIDTH]
        vn = _layernorm(z[:, GMLP_WIDTH:], ggv_ref[...], bgv_ref[...])
        if emit_vn:
            vn_ref[rs:rs + rb, :] = vn
        vb = vn.astype(BF16)
        chunks = [jnp.concatenate(
            [_dot(ws[h], vb[c0:c0 + cr, h * GMLP_HDIM:(h + 1) * GMLP_HDIM]) + bs_ref[0:cr, h:h + 1]
             for h in range(GMLP_HEADS)], axis=1) for c0 in range(0, rb, cr)]
        mixed = chunks[0] if len(chunks) == 1 else jnp.concatenate(chunks, axis=0)
        yb = _dot((u * mixed).astype(BF16), wgo_ref[...])
        merged = merged + _sigmoid(_dot(xn, win_ref[:, O_GATE + D_MODEL:O_GATE + 2 * D_MODEL])
                                   + bg_ref[1:2, :]) * yb

        cc = _dot(xn, win_ref[:, O_CONV:O_CONV + 2 * CONV_WIDTH])
        a = cc[:, :CONV_WIDTH] * _sigmoid(cc[:, CONV_WIDTH:])
        abuf[CONV_HIST_PAD + rs:CONV_HIST_PAD + rs + rb, :] = a
        c_lo = 0 if rs == 0 else rs + CONV_HIST_PAD - SUBLANES
        c_hi = rs + rb + CONV_HIST_PAD - SUBLANES
        for r in range(1, SUBLANES):
            sbuf[r - 1, c_lo:c_hi, :] = abuf[c_lo + r:c_hi + r, :]

        def tap(k, r0, n):
            q, r = divmod(CONV_HIST_PAD - CONV_HIST + k, SUBLANES)
            lo = SUBLANES * q + r0
            return abuf[lo:lo + n, :] if r == 0 else sbuf[r - 1, lo:lo + n, :]

        rc = min(CONV_ROWS, rb)
        hs = []
        for r0 in range(rs, rs + rb, rc):
            acc = tap(0, r0, rc) * wdw_ref[0:1, :]
            for k in range(1, CONV_K):
                acc = acc + tap(k, r0, rc) * wdw_ref[k:k + 1, :]
            hln = _layernorm(acc + bdw_ref[...], gcl_ref[...], bcl_ref[...])
            hs.append((hln * _sigmoid(hln)).astype(BF16))
        yc = _dot(hs[0] if len(hs) == 1 else jnp.concatenate(hs, axis=0), wco_ref[...])
        merged = merged + _sigmoid(_dot(xn, win_ref[:, O_GATE + 2 * D_MODEL:O_GATE + 3 * D_MODEL])
                                   + bg_ref[2:3, :]) * yc

        out_ref[rs:rs + rb, :] = x + _dot(merged.astype(BF16), wo_ref[...])

    p_tail = pbuf[tm:tm + POOL_HIST_PAD, :]
    a_tail = abuf[tm:tm + CONV_HIST_PAD, :]
    pbuf[0:POOL_HIST_PAD, :] = p_tail
    abuf[0:CONV_HIST_PAD, :] = a_tail
    pt_ref[...] = p_tail
    at_ref[...] = a_tail


def _mixer(x, pool_hist, conv_hist, w, *, tm, pos0, emit_vn):
    B, T, D = x.shape
    assert T % tm == 0 and tm >= CONV_HIST_PAD and (tm % GMLP_CHUNK == 0 or tm == T)
    tok = pl.BlockSpec((None, tm, D), lambda b, t: (b, t, 0))
    per_b = lambda r, c: pl.BlockSpec((None, r, c), lambda b, t: (b, 0, 0))
    weights = [w['g_mix'], w['w_in'], w['b_gate'], w['w_pool_bd'], w['s_pool'], w['g_gv'], w['b_gv'],
               w['w_s'], w['b_s_t'], w['w_gmlp_out'], w['w_dw'], w['b_dw'], w['g_cln'], w['b_cln'],
               w['w_conv_out'], w['w_out']]
    out_shape = [jax.ShapeDtypeStruct((B, T, D), F32),
                 jax.ShapeDtypeStruct((B, POOL_HIST_PAD, POOL_WIDTH), F32),
                 jax.ShapeDtypeStruct((B, CONV_HIST_PAD, CONV_WIDTH), F32)]
    out_specs = [tok, per_b(POOL_HIST_PAD, POOL_WIDTH), per_b(CONV_HIST_PAD, CONV_WIDTH)]
    if emit_vn:
        out_shape.append(jax.ShapeDtypeStruct((B, T, GMLP_WIDTH), F32))
        out_specs.append(pl.BlockSpec((None, tm, GMLP_WIDTH), lambda b, t: (b, t, 0)))
    return pl.pallas_call(
        functools.partial(_mixer_kernel, tm=tm, pos0=pos0, emit_vn=emit_vn),
        grid=(B, T // tm),
        in_specs=[tok, per_b(POOL_HIST_PAD, POOL_WIDTH), per_b(CONV_HIST_PAD, CONV_WIDTH)]
                 + [_const_spec(a.shape) for a in weights],
        out_specs=out_specs,
        out_shape=out_shape,
        scratch_shapes=[pltpu.VMEM((POOL_HIST_PAD + tm, POOL_WIDTH), F32),
                        pltpu.VMEM((CONV_HIST_PAD + tm, CONV_WIDTH), F32),
                        pltpu.VMEM((SUBLANES - 1, CONV_HIST_PAD + tm - SUBLANES, CONV_WIDTH), F32),
                        pltpu.VMEM((POOL_HIST_PAD + tm, POOL_WIDTH), F32),
                        pltpu.VMEM((POOL_HIST_PAD + tm, POOL_WIDTH), F32),
                        pltpu.VMEM((POOL_HIST_PAD + tm, POOL_WIDTH // 2), F32)],
        compiler_params=pltpu.CompilerParams(dimension_semantics=("arbitrary", "arbitrary"),
                                             vmem_limit_bytes=VMEM_LIMIT_BYTES),
        name="mixer",
    )(x, pool_hist, conv_hist, *weights)


def _kv_kernel(m_ref, g_ref, wk_ref, wv_ref, k_ref, v_ref):
    mn = _rmsnorm(m_ref[...], g_ref[...]).astype(BF16)
    k_ref[...] = _dot(mn, wk_ref[...])
    v_ref[...] = _dot(mn, wv_ref[...])


def _memory_kv(mem, w, *, tm):
    B, M, D = mem.shape
    rows = mem.reshape(B * M, D)
    blk = pl.BlockSpec((tm, D), lambda i: (i, 0))
    k, v = pl.pallas_call(
        _kv_kernel,
        grid=(B * M // tm,),
        in_specs=[blk, _const_spec((1, D)), _const_spec((D, D)), _const_spec((D, D))],
        out_specs=[blk, blk],
        out_shape=[jax.ShapeDtypeStruct((B * M, D), F32)] * 2,
        compiler_params=pltpu.CompilerParams(dimension_semantics=("arbitrary",),
                                             vmem_limit_bytes=VMEM_LIMIT_BYTES),
        name="memory_kv",
    )(rows, w['g_mem'], w['w_xk'], w['w_xv'])
    return k.reshape(B, M, D), v.reshape(B, M, D)


def _xattn_kernel(x_ref, k_ref, v_ref, g_ref, wq_ref, wo_ref, *rest, route):
    if route:
        gf_ref, wrc_ref, br_ref, out_ref, bkt_ref, kb, vb = rest
    else:
        out_ref, kb, vb = rest

    @pl.when(pl.program_id(1) == 0)
    def _():
        kb[...] = k_ref[...].astype(BF16)
        vb[...] = v_ref[...].astype(BF16)

    x = x_ref[...]
    xn = _rmsnorm(x, g_ref[...]).astype(BF16)
    q = (_dot(xn, wq_ref[...]) * (XA_HDIM ** -0.5)).astype(BF16)
    heads = []
    for h in range(XA_HEADS):
        sl = slice(h * XA_HDIM, (h + 1) * XA_HDIM)
        s = lax.dot_general(q[:, sl], kb[:, sl], (((1,), (1,)), ((), ())), preferred_element_type=F32)
        e = jnp.exp(s - jnp.max(s, axis=-1, keepdims=True))
        pr = (e / jnp.sum(e, axis=-1, keepdims=True)).astype(BF16)
        heads.append(_dot(pr, vb[:, sl]).astype(BF16))
    o = jnp.concatenate(heads, axis=1)
    y = x + _dot(o, wo_ref[...])
    out_ref[...] = y
    if route:
        lane, gsel, _, i1, i2, _, _ = _route_select(_rmsnorm(y, gf_ref[...]), wrc_ref, br_ref)
        e_lo = LANE_E0 + gsel * EXP_PER_GROUP
        a = jnp.minimum(i1, i2) - e_lo
        b = jnp.maximum(i1, i2) - e_lo
        pair = jnp.where(a == 0, 0, jnp.where(a == 1, 3, 5)) + (b - a - 1)
        bkt_ref[...] = jnp.broadcast_to((gsel * PAIRS_PER_GROUP + pair).astype(F32), bkt_ref.shape)


def _cross_attn(x, k, v, w, *, tm, route):
    B, T, D = x.shape
    tok = pl.BlockSpec((None, tm, D), lambda b, t: (b, t, 0))
    mem = pl.BlockSpec((None, MEM_LEN, D), lambda b, t: (b, 0, 0))
    in_specs = [tok, mem, mem, _const_spec((1, D)), _const_spec((D, D)), _const_spec((D, D))]
    args = [x, k, v, w['g_xa'], w['w_xq'], w['w_xo']]
    out_specs = [tok]
    out_shape = [jax.ShapeDtypeStruct((B, T, D), F32)]
    if route:
        in_specs += [_const_spec((1, D)), _const_spec((D, 2 * ROUTER_LANES)), _const_spec((1, ROUTER_LANES))]
        args += [w['g_ffn'], w['wr_cat'], w['b_r']]
        out_specs.append(pl.BlockSpec((None, tm, ROUTER_LANES), lambda b, t: (b, t, 0)))
        out_shape.append(jax.ShapeDtypeStruct((B, T, ROUTER_LANES), F32))
    res = pl.pallas_call(
        functools.partial(_xattn_kernel, route=route),
        grid=(B, T // tm),
        in_specs=in_specs,
        out_specs=out_specs,
        out_shape=out_shape,
        scratch_shapes=[pltpu.VMEM((MEM_LEN, D), BF16), pltpu.VMEM((MEM_LEN, D), BF16)],
        compiler_params=pltpu.CompilerParams(dimension_semantics=("arbitrary", "arbitrary"),
                                             vmem_limit_bytes=VMEM_LIMIT_BYTES),
        name="cross_attn_route" if route else "cross_attn",
    )(*args)
    return res if route else res[0]


def _route_select(xn, wr_cat_ref, br_ref):
    x_hi = xn.astype(BF16)
    x_lo = (xn - x_hi.astype(F32)).astype(BF16)
    hh_hl = _dot(x_hi, wr_cat_ref[...])
    logits = (hh_hl[:, :ROUTER_LANES] + hh_hl[:, ROUTER_LANES:]
              + _dot(x_lo, wr_cat_ref[:, :ROUTER_LANES]) + br_ref[...])
    lane = lax.broadcasted_iota(jnp.int32, logits.shape, 1)
    neg = jnp.float32(-jnp.inf)
    big = jnp.int32(ROUTER_LANES)

    def first_argmax(vals):
        m = jnp.max(vals, axis=-1, keepdims=True)
        return m, jnp.min(jnp.where(vals == m, lane, big), axis=-1, keepdims=True)

    gl = jnp.where(lane < N_GROUPS, logits, neg)
    gmax, gsel = first_argmax(gl)
    pg = 1.0 / jnp.sum(jnp.exp(gl - gmax), axis=-1, keepdims=True)
    e_lo = LANE_E0 + gsel * EXP_PER_GROUP
    el = jnp.where((lane >= e_lo) & (lane < e_lo + EXP_PER_GROUP), logits, neg)
    v1, i1 = first_argmax(el)
    v2, i2 = first_argmax(jnp.where(lane == i1, neg, el))
    return lane, gsel, pg, i1, i2, v1, v2


def _route(xn, wr_cat_ref, br_ref):
    lane, _, pg, i1, i2, v1, v2 = _route_select(xn, wr_cat_ref, br_ref)
    r = jnp.exp(v2 - v1)
    pe1 = pg / (1.0 + r)
    pe2 = pg * r / (1.0 + r)
    return jnp.where(lane == i1, pe1, 0.0) + jnp.where(lane == i2, pe2, 0.0)


def _moe_dense_kernel(x_ref, g_ref, wrc_ref, br_ref, wgu_ref, wd_ref, gf_ref, out_ref, he_buf,
                      *, final_norm):
    x = x_ref[...]
    xn = _rmsnorm(x, g_ref[...])
    comb = _route(xn, wrc_ref, br_ref)
    xb = xn.astype(BF16)
    lane = lax.broadcasted_iota(jnp.int32, comb.shape, 1)
    for e in range(N_EXPERTS):
        c = jnp.sum(jnp.where(lane == e + LANE_E0, comb, 0.0), axis=-1, keepdims=True)
        h = _dot(xb, wgu_ref[e])
        hg = h[:, :D_EXPERT]
        he = c * (hg * _sigmoid(hg) * h[:, D_EXPERT:])
        he_buf[:, e * D_EXPERT:(e + 1) * D_EXPERT] = he.astype(BF16)
    y = x + _dot(he_buf[...], wd_ref[...])
    out_ref[...] = _rmsnorm(y, gf_ref[...]) if final_norm else y


def _moe_dense(x, w, g_final, *, tm, final_norm):
    B, T, D = x.shape
    rows = x.reshape(B * T, D)
    tok = pl.BlockSpec((tm, D), lambda i: (i, 0))
    out = pl.pallas_call(
        functools.partial(_moe_dense_kernel, final_norm=final_norm),
        grid=(B * T // tm,),
        in_specs=[tok, _const_spec((1, D)), _const_spec((D, 2 * ROUTER_LANES)),
                  _const_spec((1, ROUTER_LANES)), _const_spec((N_EXPERTS, D, 2 * D_EXPERT)),
                  _const_spec((N_EXPERTS * D_EXPERT, D)), _const_spec((1, D))],
        out_specs=tok,
        out_shape=jax.ShapeDtypeStruct((B * T, D), F32),
        scratch_shapes=[pltpu.VMEM((tm, N_EXPERTS * D_EXPERT), BF16)],
        compiler_params=pltpu.CompilerParams(dimension_semantics=("arbitrary",),
                                             vmem_limit_bytes=VMEM_LIMIT_BYTES),
        name="moe_dense",
    )(rows, w['g_ffn'], w['wr_cat'], w['b_r'], jnp.concatenate([w['w_eg'], w['w_eu']], axis=-1),
      w['w_ed'].reshape(N_EXPERTS * D_EXPERT, D), g_final)
    return out.reshape(B, T, D)


def _moe_sorted_kernel(tile_ref, ea_ref, eb_ref, grp_ref, lo_ref, hi_ref, flag_ref,
                       sprev_ref, scur_ref, snext_ref, x_hbm, g_ref, wrh_ref, br_ref,
                       wga_ref, wua_ref, wda_ref, wgb_ref, wub_ref, wdb_ref, gf_ref,
                       y_hbm, xb0, xb1, ob0, ob1, gsem, ssem, *, tmg, n_tiles, final_norm):
    v = pl.program_id(0)
    tile, lo, hi, flags = tile_ref[v], lo_ref[v], hi_ref[v], flag_ref[v]
    xb, ob = (xb0, xb1), (ob0, ob1)
    first = (flags & 1) == 1
    last = (flags & 2) == 2
    has_next = tile < n_tiles - 1
    has_prev = tile > 0
    p_last = (n_tiles - 1) % 2

    def gather_start(idx_ref, k, p, off=0):
        pltpu.make_async_copy(x_hbm.at[pl.ds(idx_ref[0, k] + off, 1)], xb[p].at[pl.ds(k, 1)], gsem.at[p]).start()

    def scatter_start(idx_ref, k, p, off=0):
        pltpu.make_async_copy(ob[p].at[pl.ds(k, 1)], y_hbm.at[pl.ds(idx_ref[0, k] + off, 1)], ssem.at[p]).start()

    def gather_wait(p):
        pltpu.make_async_copy(x_hbm.at[pl.ds(0, tmg)], xb[p], gsem.at[p]).wait()

    def scatter_wait(p):
        pltpu.make_async_copy(ob[p], y_hbm.at[pl.ds(0, tmg)], ssem.at[p]).wait()

    def on_parity(cond, fn):
        for p in (0, 1):
            pl.when(cond & (tile % 2 == p))(functools.partial(fn, p))

    @pl.when(v == 0)
    def _():
        for k in range(tmg):
            gather_start(scur_ref, k, 0)

    def begin_tile(p):
        gather_wait(p)

        @pl.when(tile >= 2)
        def _():
            scatter_wait(p)

        ob[p][...] = jnp.zeros((tmg, D_MODEL), F32)

    on_parity(first, begin_tile)

    def compute(overlap_dma, p):
        pending = []
        if overlap_dma:
            for k in range(tmg):
                pending.append((gather_start, snext_ref, k))
                pending.append((scatter_start, sprev_ref, k))
        n_cols = D_MODEL // D_EXPERT
        per_batch = -(-len(pending) // (2 + 2 * (2 + n_cols)))

        def issue_after(val):
            if not pending:
                return
            bits = pltpu.bitcast(val[0:8, 0:128], jnp.int32)[0, 0]
            zero = lax.shift_right_logical(bits & jnp.int32(0x7FFFFFFF), jnp.int32(31))
            for _ in range(min(per_batch, len(pending))):
                start, idx_ref, k = pending.pop(0)
                start(idx_ref, k, 1 - p, zero)

        x = xb[p][...]
        xn = _rmsnorm(x, g_ref[...]).astype(BF16)
        logits = _dot(xn, wrh_ref[...]) + br_ref[...]
        issue_after(x)
        lane = lax.broadcasted_iota(jnp.int32, logits.shape, 1)
        pick = lambda l: jnp.sum(jnp.where(lane == l, logits, 0.0), axis=-1, keepdims=True)
        gl = jnp.where(lane < N_GROUPS, logits, jnp.float32(-jnp.inf))
        gmax = jnp.max(gl, axis=-1, keepdims=True)
        pg = jnp.exp(pick(grp_ref[v]) - gmax) / jnp.sum(jnp.exp(gl - gmax), axis=-1, keepdims=True)
        va, vb = pick(LANE_E0 + ea_ref[v]), pick(LANE_E0 + eb_ref[v])
        row = lax.broadcasted_iota(jnp.int32, (tmg, 1), 0)
        inside = (row >= lo) & (row < hi)
        prev = logits
        hes = []
        for wt, wg_ref, wu_ref in ((pg / (1.0 + jnp.exp(vb - va)), wga_ref, wua_ref),
                                   (pg / (1.0 + jnp.exp(va - vb)), wgb_ref, wub_ref)):
            hg = _dot(xn, wg_ref[...])
            issue_after(prev)
            hu = _dot(xn, wu_ref[...])
            issue_after(hg)
            prev = hu
            hes.append(jnp.where(inside, wt * (hg * _sigmoid(hg) * hu), 0.0).astype(BF16))
        for c in range(0, D_MODEL, D_EXPERT):
            cols = slice(c, c + D_EXPERT)
            da = _dot(hes[0], wda_ref[:, cols])
            issue_after(prev)
            db = _dot(hes[1], wdb_ref[:, cols])
            issue_after(da)
            prev = db
            ob[p][:, cols] = ob[p][:, cols] + da + db
        issue_after(prev)
        assert not pending

    interior = first & has_next & has_prev
    on_parity(interior, functools.partial(compute, True))

    @pl.when(first & has_next & ~has_prev)
    def _():
        for k in range(tmg):
            gather_start(snext_ref, k, 1)

    @pl.when(first & ~has_next & has_prev)
    def _():
        for k in range(tmg):
            scatter_start(sprev_ref, k, 1 - p_last)

    on_parity((first & ~interior) | (~first & (hi > lo)), functools.partial(compute, False))

    def end_tile(p):
        y = xb[p][...] + ob[p][...]
        ob[p][...] = _rmsnorm(y, gf_ref[...]) if final_norm else y

    on_parity(last, end_tile)

    @pl.when(v == pl.num_programs(0) - 1)
    def _():
        for k in range(tmg):
            scatter_start(scur_ref, k, p_last)
        if n_tiles > 1:
            scatter_wait(1 - p_last)
        scatter_wait(p_last)


def _moe_plan(bucket, n, tmg):
    i32 = jnp.int32
    src = jnp.argsort(bucket, stable=True).astype(i32)
    cnt = jnp.sum((bucket[:, None] == jnp.arange(N_BUCKETS, dtype=i32)[None, :]).astype(i32), axis=0)
    ends = jnp.cumsum(cnt)
    offs = ends - cnt
    nt = n // tmg
    cuts = jnp.sort(jnp.concatenate([jnp.arange(nt, dtype=i32) * tmg, offs[1:]]))
    nxt = jnp.concatenate([cuts[1:], jnp.full((1,), n, i32)])
    tile = jnp.minimum(cuts // tmg, nt - 1)
    bkt = jnp.minimum(jnp.sum((ends[None, :] <= cuts[:, None]).astype(i32), axis=1), N_BUCKETS - 1)
    change = (tile[1:] != tile[:-1]).astype(i32)
    one = jnp.ones((1,), i32)
    flags = jnp.concatenate([one, change]) + 2 * jnp.concatenate([change, one])
    grp = bkt // PAIRS_PER_GROUP
    pair = bkt % PAIRS_PER_GROUP
    ea = grp * EXP_PER_GROUP + jnp.take(jnp.array([0, 0, 0, 1, 1, 2], i32), pair)
    eb = grp * EXP_PER_GROUP + jnp.take(jnp.array([1, 2, 3, 2, 3, 3], i32), pair)
    return src, (tile, ea, eb, grp, cuts - tile * tmg, nxt - tile * tmg, flags)


def _moe_sorted(x, bucket, w, g_final, *, tmg, final_norm):
    B, T, D = x.shape
    n = B * T
    nt = n // tmg
    assert n % tmg == 0
    src, tables = _moe_plan(bucket.reshape(n, ROUTER_LANES)[:, 0].astype(jnp.int32), n, tmg)
    src3 = src.reshape(nt, 1, tmg)
    idx = lambda d: pl.BlockSpec((None, 1, tmg), lambda v, tile, *_: (jnp.clip(tile[v] + d, 0, nt - 1), 0, 0),
                                 memory_space=pltpu.SMEM)
    const = lambda shape: pl.BlockSpec(shape, lambda v, *_: (0,) * len(shape), pipeline_mode=pl.Buffered(1))
    w_a = lambda r, c: pl.BlockSpec((None, r, c), lambda v, tile, ea, eb, *_: (ea[v], 0, 0))
    w_b = lambda r, c: pl.BlockSpec((None, r, c), lambda v, tile, ea, eb, *_: (eb[v], 0, 0))
    any_spec = pl.BlockSpec(memory_space=pl.ANY)
    y = pl.pallas_call(
        functools.partial(_moe_sorted_kernel, tmg=tmg, n_tiles=nt, final_norm=final_norm),
        grid_spec=pltpu.PrefetchScalarGridSpec(
            num_scalar_prefetch=len(tables),
            grid=(tables[0].shape[0],),
            in_specs=[idx(-1), idx(0), idx(1), any_spec, const((1, D)), const((D, ROUTER_LANES)),
                      const((1, ROUTER_LANES)),
                      w_a(D, D_EXPERT), w_a(D, D_EXPERT), w_a(D_EXPERT, D),
                      w_b(D, D_EXPERT), w_b(D, D_EXPERT), w_b(D_EXPERT, D), const((1, D))],
            out_specs=any_spec,
            scratch_shapes=[pltpu.VMEM((tmg, D), F32)] * 4
                           + [pltpu.SemaphoreType.DMA((2,)), pltpu.SemaphoreType.DMA((2,))]),
        out_shape=jax.ShapeDtypeStruct((n, D), F32),
        compiler_params=pltpu.CompilerParams(dimension_semantics=("arbitrary",),
                                             vmem_limit_bytes=VMEM_LIMIT_BYTES, has_side_effects=True),
        name="moe_sorted",
    )(*tables, src3, src3, src3, x.reshape(n, D), w['g_ffn'], w['wr_hi'], w['b_r'],
      w['w_eg'], w['w_eu'], w['w_ed'], w['w_eg'], w['w_eu'], w['w_ed'], g_final)
    return y.reshape(B, T, D)


def _layer_weights(l, g_mix, w_in, b_gate, w_pool, s_pool, g_gv, b_gv, w_s, b_s, w_gmlp_out,
                   w_dw, b_dw, g_cln, b_cln, w_conv_out, w_out, g_xa, g_mem, w_xq, w_xk, w_xv, w_xo,
                   g_ffn, w_rg, b_rg, w_re, b_re, w_eg, w_eu, w_ed):
    row = lambda a: a[l].reshape(1, -1)
    wp = jnp.zeros((POOL_WIDTH, D_MODEL), F32)
    gout = D_MODEL // len(POOL_WINDOWS)
    for g in range(len(POOL_WINDOWS)):
        wp = wp.at[g * POOL_GDIM:(g + 1) * POOL_GDIM, g * gout:(g + 1) * gout].set(w_pool[l, g])
    wr = jnp.zeros((D_MODEL, ROUTER_LANES), F32)
    wr = wr.at[:, :N_GROUPS].set(w_rg[l]).at[:, LANE_E0:LANE_E0 + N_EXPERTS].set(w_re[l])
    wr_hi = wr.astype(BF16)
    br = jnp.zeros((1, ROUTER_LANES), F32)
    br = br.at[0, :N_GROUPS].set(b_rg[l]).at[0, LANE_E0:LANE_E0 + N_EXPERTS].set(b_re[l])
    return {
        'g_mix': row(g_mix), 'w_in': w_in[l].astype(BF16), 'b_gate': b_gate[l],
        'w_pool_bd': wp.astype(BF16), 's_pool': row(s_pool), 'g_gv': row(g_gv), 'b_gv': row(b_gv),
        'w_s': w_s[l], 'b_s_t': jnp.transpose(b_s[l]), 'w_gmlp_out': w_gmlp_out[l].astype(BF16),
        'w_dw': w_dw[l], 'b_dw': row(b_dw), 'g_cln': row(g_cln), 'b_cln': row(b_cln),
        'w_conv_out': w_conv_out[l].astype(BF16), 'w_out': w_out[l].astype(BF16),
        'g_xa': row(g_xa), 'g_mem': row(g_mem), 'w_xq': w_xq[l].astype(BF16),
        'w_xk': w_xk[l].astype(BF16), 'w_xv': w_xv[l].astype(BF16), 'w_xo': w_xo[l].astype(BF16),
        'g_ffn': row(g_ffn), 'wr_hi': wr_hi, 'b_r': br,
        'wr_cat': jnp.concatenate([wr_hi, (wr - wr_hi.astype(F32)).astype(BF16)], axis=1),
        'w_eg': w_eg[l].astype(BF16), 'w_eu': w_eu[l].astype(BF16), 'w_ed': w_ed[l].astype(BF16),
    }


def _attn_moe(x, k, v, w, g_final, *, tm, final_norm):
    n = x.shape[0] * x.shape[1]
    if n >= MOE_SORTED_MIN_ROWS:
        x, bucket = _cross_attn(x, k, v, w, tm=tm, route=True)
        return _moe_sorted(x, bucket, w, g_final, tmg=MOE_SORTED_TILE, final_norm=final_norm)
    x = _cross_attn(x, k, v, w, tm=tm, route=False)
    return _moe_dense(x, w, g_final, tm=_token_tile(n), final_norm=final_norm)


def _pad_hist(h, rows):
    return jnp.pad(h, ((0, 0), (rows - h.shape[1], 0), (0, 0)))


def _token_tile(t):
    return min(1024, t)


def kernel(x_prompt, x_sample, mem_prompt, cache_pool, cache_conv, cache_mem_k, cache_mem_v, g_mix, w_in, b_gate, w_pool, s_pool, g_gv, b_gv, w_s, b_s, w_gmlp_out, w_dw, b_dw, g_cln, b_cln, w_conv_out, w_out, g_xa, g_mem, w_xq, w_xk, w_xv, w_xo, g_ffn, w_rg, b_rg, w_re, b_re, w_eg, w_eu, w_ed, g_final):
    depth = w_in.shape[0]
    bp, tp, _ = x_prompt.shape
    bs, ts, _ = x_sample.shape
    tmp, tms = _token_tile(tp), _token_tile(ts)
    gf = g_final.reshape(1, -1)
    xp, xs = x_prompt, x_sample
    pool0 = jnp.zeros((bp, POOL_HIST_PAD, POOL_WIDTH), F32)
    conv0 = jnp.zeros((bp, CONV_HIST_PAD, CONV_WIDTH), F32)
    outs = {k: [] for k in ('pp', 'pc', 'pk', 'pv', 'sp', 'sc', 'sv')}
    for l in range(depth):
        w = _layer_weights(l, g_mix, w_in, b_gate, w_pool, s_pool, g_gv, b_gv, w_s, b_s, w_gmlp_out,
                           w_dw, b_dw, g_cln, b_cln, w_conv_out, w_out, g_xa, g_mem, w_xq, w_xk, w_xv,
                           w_xo, g_ffn, w_rg, b_rg, w_re, b_re, w_eg, w_eu, w_ed)
        last = l == depth - 1
        xp, ph, ch = _mixer(xp, pool0, conv0, w, tm=tmp, pos0=0, emit_vn=False)
        mk, mv = _memory_kv(mem_prompt, w, tm=_token_tile(bp * MEM_LEN))
        xp = _attn_moe(xp, mk, mv, w, gf, tm=tmp, final_norm=last)
        outs['pp'].append(ph[:, POOL_HIST_PAD - POOL_HIST:])
        outs['pc'].append(ch[:, CONV_HIST_PAD - CONV_HIST:])
        outs['pk'].append(mk.reshape(bp, MEM_LEN, XA_HEADS, XA_HDIM))
        outs['pv'].append(mv.reshape(bp, MEM_LEN, XA_HEADS, XA_HDIM))
        xs, sh, sc, vn = _mixer(xs, _pad_hist(cache_pool[l], POOL_HIST_PAD),
                                _pad_hist(cache_conv[l], CONV_HIST_PAD), w,
                                tm=tms, pos0=PAST_LEN, emit_vn=True)
        xs = _attn_moe(xs, cache_mem_k[l].reshape(bs, MEM_LEN, D_MODEL),
                       cache_mem_v[l].reshape(bs, MEM_LEN, D_MODEL), w, gf, tm=tms, final_norm=last)
        outs['sp'].append(sh[:, POOL_HIST_PAD - POOL_HIST:])
        outs['sc'].append(sc[:, CONV_HIST_PAD - CONV_HIST:])
        outs['sv'].append(vn)
    st = lambda k: jnp.stack(outs[k], axis=0)
    return (xp, xs, st('pp'), st('pc'), st('pk'), st('pv'), st('sp'), st('sc'), st('sv'))
```

```python
import functools

import jax
import jax.numpy as jnp
from jax import lax
from jax.experimental import pallas as pl
from jax.experimental.pallas import tpu as pltpu

F32 = jnp.float32
BF16 = jnp.bfloat16

EPS = 1e-6
D_MODEL = 1024
PAST_LEN = 4096
CHUNK = 64
POOL_WIDTH = 256
POOL_WINDOWS = (2, 4, 8, 16)
POOL_GDIM = 64
POOL_HIST = 15
POOL_HIST_PAD = 32
GMLP_WIDTH = 512
GMLP_HEADS = 4
GMLP_HDIM = 128
GMLP_CHUNK = 128
CONV_WIDTH = 256
CONV_K = 31
CONV_HIST = 30
CONV_HIST_PAD = 32
SUBLANES = 8
CONV_ROWS = 64
N_BRANCH = 3
O_POOL = 0
O_GMLP = POOL_WIDTH
O_CONV = O_GMLP + 2 * GMLP_WIDTH
O_GATE = O_CONV + 2 * CONV_WIDTH
IN_COLS = O_GATE + N_BRANCH * D_MODEL
MEM_LEN = 256
XA_HEADS = 4
XA_HDIM = 256
N_GROUPS = 4
EXP_PER_GROUP = 4
N_EXPERTS = 16
D_EXPERT = 256
ROUTER_LANES = 128
LANE_E0 = N_GROUPS
PAIRS_PER_GROUP = 6
N_BUCKETS = N_GROUPS * PAIRS_PER_GROUP
MOE_SORTED_TILE = 256
MOE_SORTED_MIN_ROWS = 1024

VMEM_LIMIT_BYTES = 56 * 1024 * 1024


def _dot(a, b):
    return jnp.dot(a, b, preferred_element_type=F32)


def _rmsnorm(x, g):
    return x * lax.rsqrt(jnp.mean(x * x, axis=-1, keepdims=True) + EPS) * g


def _layernorm(x, g, b):
    mu = jnp.mean(x, axis=-1, keepdims=True)
    xc = x - mu
    var = jnp.mean(xc * xc, axis=-1, keepdims=True)
    return xc * lax.rsqrt(var + EPS) * g + b


def _sigmoid(x):
    return 1.0 / (1.0 + jnp.exp(-x))


def _const_spec(shape):
    nd = len(shape)
    return pl.BlockSpec(shape, lambda *_: (0,) * nd, pipeline_mode=pl.Buffered(1))


def _mixer_kernel(x_ref, ph_ref, ch_ref, g_ref, win_ref, bg_ref, wp_ref, sp_ref, ggv_ref, bgv_ref,
                  ws_ref, bs_ref, wgo_ref, wdw_ref, bdw_ref, gcl_ref, bcl_ref, wco_ref, wo_ref,
                  *rest, tm, pos0, emit_vn):
    if emit_vn:
        out_ref, pt_ref, at_ref, vn_ref, pbuf, abuf, sbuf, s2buf, s4buf, s8buf = rest
    else:
        out_ref, pt_ref, at_ref, pbuf, abuf, sbuf, s2buf, s4buf, s8buf = rest
        vn_ref = None
    t = pl.program_id(1)

    @pl.when(t == 0)
    def _():
        pbuf[0:POOL_HIST_PAD, :] = ph_ref[...]
        abuf[0:CONV_HIST_PAD, :] = ch_ref[...]

    cr = min(GMLP_CHUNK, tm)
    bi = lax.broadcasted_iota(jnp.int32, (cr, cr), 0) // CHUNK
    bj = lax.broadcasted_iota(jnp.int32, (cr, cr), 1) // CHUNK
    ws = [jnp.where(bi >= bj, ws_ref[h, 0:cr, 0:cr], 0.0).astype(BF16) for h in range(GMLP_HEADS)]
    lane = lax.broadcasted_iota(jnp.int32, (tm, 128), 1)

    x = x_ref[...]
    xn = _rmsnorm(x, g_ref[...]).astype(BF16)

    p = _dot(xn, win_ref[:, O_POOL:O_POOL + POOL_WIDTH])
    pbuf[POOL_HIST_PAD:POOL_HIST_PAD + tm, :] = p
    pos = (pos0 + t * tm + lax.broadcasted_iota(jnp.int32, (tm, 1), 0) + 1).astype(F32)
    lo2, hi = SUBLANES, POOL_HIST_PAD + tm
    s2buf[lo2:hi, :] = pbuf[lo2:hi, :] + pbuf[lo2 - 1:hi - 1, :]
    s4buf[lo2 + 8:hi, :] = s2buf[lo2 + 8:hi, :] + s2buf[lo2 + 6:hi - 2, :]
    s8buf[lo2 + 16:hi, :] = s4buf[lo2 + 16:hi, 128:256] + s4buf[lo2 + 12:hi - 4, 128:256]
    r0 = POOL_HIST_PAD
    s8 = s8buf[r0:hi, :]
    s16 = s8 + s8buf[r0 - 8:hi - 8, :]
    w2, w4, w8, w16 = (jnp.minimum(float(w), pos) for w in POOL_WINDOWS)
    means = [jnp.where(lane < POOL_GDIM, s2buf[r0:hi, 0:128] / w2, s4buf[r0:hi, 0:128] / w4),
             jnp.where(lane < POOL_GDIM, s8 / w8, s16 / w16)]
    d = (jnp.concatenate(means, axis=1) - p).astype(BF16)
    ya = _dot(d, wp_ref[...]) * sp_ref[...]
    merged = _sigmoid(_dot(xn, win_ref[:, O_GATE:O_GATE + D_MODEL]) + bg_ref[0:1, :]) * ya

    z = jax.nn.gelu(_dot(xn, win_ref[:, O_GMLP:O_GMLP + 2 * GMLP_WIDTH]))
    u = z[:, :GMLP_WIDTH]
    vn = _layernorm(z[:, GMLP_WIDTH:], ggv_ref[...], bgv_ref[...])
    if emit_vn:
        vn_ref[...] = vn
    vb = vn.astype(BF16)
    chunks = [jnp.concatenate(
        [_dot(ws[h], vb[c0:c0 + cr, h * GMLP_HDIM:(h + 1) * GMLP_HDIM]) + bs_ref[0:cr, h:h + 1]
         for h in range(GMLP_HEADS)], axis=1) for c0 in range(0, tm, cr)]
    mixed = chunks[0] if len(chunks) == 1 else jnp.concatenate(chunks, axis=0)
    yb = _dot((u * mixed).astype(BF16), wgo_ref[...])
    merged = merged + _sigmoid(_dot(xn, win_ref[:, O_GATE + D_MODEL:O_GATE + 2 * D_MODEL])
                               + bg_ref[1:2, :]) * yb

    cc = _dot(xn, win_ref[:, O_CONV:O_CONV + 2 * CONV_WIDTH])
    a = cc[:, :CONV_WIDTH] * _sigmoid(cc[:, CONV_WIDTH:])
    abuf[CONV_HIST_PAD:CONV_HIST_PAD + tm, :] = a
    span = tm + CONV_HIST_PAD - SUBLANES
    for r in range(1, SUBLANES):
        sbuf[r - 1, :, :] = abuf[r:r + span, :]

    def tap(k, r0, n):
        q, r = divmod(CONV_HIST_PAD - CONV_HIST + k, SUBLANES)
        lo = SUBLANES * q + r0
        return abuf[lo:lo + n, :] if r == 0 else sbuf[r - 1, lo:lo + n, :]

    rc = min(CONV_ROWS, tm)
    hs = []
    for r0 in range(0, tm, rc):
        acc = tap(0, r0, rc) * wdw_ref[0:1, :]
        for k in range(1, CONV_K):
            acc = acc + tap(k, r0, rc) * wdw_ref[k:k + 1, :]
        hln = _layernorm(acc + bdw_ref[...], gcl_ref[...], bcl_ref[...])
        hs.append((hln * _sigmoid(hln)).astype(BF16))
    yc = _dot(hs[0] if len(hs) == 1 else jnp.concatenate(hs, axis=0), wco_ref[...])
    merged = merged + _sigmoid(_dot(xn, win_ref[:, O_GATE + 2 * D_MODEL:O_GATE + 3 * D_MODEL])
                               + bg_ref[2:3, :]) * yc

    out_ref[...] = x + _dot(merged.astype(BF16), wo_ref[...])

    p_tail = pbuf[tm:tm + POOL_HIST_PAD, :]
    a_tail = abuf[tm:tm + CONV_HIST_PAD, :]
    pbuf[0:POOL_HIST_PAD, :] = p_tail
    abuf[0:CONV_HIST_PAD, :] = a_tail
    pt_ref[...] = p_tail
    at_ref[...] = a_tail


def _mixer(x, pool_hist, conv_hist, w, *, tm, pos0, emit_vn):
    B, T, D = x.shape
    assert T % tm == 0 and tm >= CONV_HIST_PAD and (tm % GMLP_CHUNK == 0 or tm == T)
    tok = pl.BlockSpec((None, tm, D), lambda b, t: (b, t, 0))
    per_b = lambda r, c: pl.BlockSpec((None, r, c), lambda b, t: (b, 0, 0))
    weights = [w['g_mix'], w['w_in'], w['b_gate'], w['w_pool_bd'], w['s_pool'], w['g_gv'], w['b_gv'],
               w['w_s'], w['b_s_t'], w['w_gmlp_out'], w['w_dw'], w['b_dw'], w['g_cln'], w['b_cln'],
               w['w_conv_out'], w['w_out']]
    out_shape = [jax.ShapeDtypeStruct((B, T, D), F32),
                 jax.ShapeDtypeStruct((B, POOL_HIST_PAD, POOL_WIDTH), F32),
                 jax.ShapeDtypeStruct((B, CONV_HIST_PAD, CONV_WIDTH), F32)]
    out_specs = [tok, per_b(POOL_HIST_PAD, POOL_WIDTH), per_b(CONV_HIST_PAD, CONV_WIDTH)]
    if emit_vn:
        out_shape.append(jax.ShapeDtypeStruct((B, T, GMLP_WIDTH), F32))
        out_specs.append(pl.BlockSpec((None, tm, GMLP_WIDTH), lambda b, t: (b, t, 0)))
    return pl.pallas_call(
        functools.partial(_mixer_kernel, tm=tm, pos0=pos0, emit_vn=emit_vn),
        grid=(B, T // tm),
        in_specs=[tok, per_b(POOL_HIST_PAD, POOL_WIDTH), per_b(CONV_HIST_PAD, CONV_WIDTH)]
                 + [_const_spec(a.shape) for a in weights],
        out_specs=out_specs,
        out_shape=out_shape,
        scratch_shapes=[pltpu.VMEM((POOL_HIST_PAD + tm, POOL_WIDTH), F32),
                        pltpu.VMEM((CONV_HIST_PAD + tm, CONV_WIDTH), F32),
                        pltpu.VMEM((SUBLANES - 1, CONV_HIST_PAD + tm - SUBLANES, CONV_WIDTH), F32),
                        pltpu.VMEM((POOL_HIST_PAD + tm, POOL_WIDTH), F32),
                        pltpu.VMEM((POOL_HIST_PAD + tm, POOL_WIDTH), F32),
                        pltpu.VMEM((POOL_HIST_PAD + tm, POOL_WIDTH // 2), F32)],
        compiler_params=pltpu.CompilerParams(dimension_semantics=("arbitrary", "arbitrary"),
                                             vmem_limit_bytes=VMEM_LIMIT_BYTES),
        name="mixer",
    )(x, pool_hist, conv_hist, *weights)


def _kv_kernel(m_ref, g_ref, wk_ref, wv_ref, k_ref, v_ref):
    mn = _rmsnorm(m_ref[...], g_ref[...]).astype(BF16)
    k_ref[...] = _dot(mn, wk_ref[...])
    v_ref[...] = _dot(mn, wv_ref[...])


def _memory_kv(mem, w, *, tm):
    B, M, D = mem.shape
    rows = mem.reshape(B * M, D)
    blk = pl.BlockSpec((tm, D), lambda i: (i, 0))
    k, v = pl.pallas_call(
        _kv_kernel,
        grid=(B * M // tm,),
        in_specs=[blk, _const_spec((1, D)), _const_spec((D, D)), _const_spec((D, D))],
        out_specs=[blk, blk],
        out_shape=[jax.ShapeDtypeStruct((B * M, D), F32)] * 2,
        compiler_params=pltpu.CompilerParams(dimension_semantics=("arbitrary",),
                                             vmem_limit_bytes=VMEM_LIMIT_BYTES),
        name="memory_kv",
    )(rows, w['g_mem'], w['w_xk'], w['w_xv'])
    return k.reshape(B, M, D), v.reshape(B, M, D)


def _xattn_kernel(x_ref, k_ref, v_ref, g_ref, wq_ref, wo_ref, *rest, route):
    if route:
        gf_ref, wrc_ref, br_ref, out_ref, bkt_ref, kb, vb = rest
    else:
        out_ref, kb, vb = rest

    @pl.when(pl.program_id(1) == 0)
    def _():
        kb[...] = k_ref[...].astype(BF16)
        vb[...] = v_ref[...].astype(BF16)

    x = x_ref[...]
    xn = _rmsnorm(x, g_ref[...]).astype(BF16)
    q = _dot(xn, wq_ref[...]).astype(BF16)
    heads = []
    for h in range(XA_HEADS):
        sl = slice(h * XA_HDIM, (h + 1) * XA_HDIM)
        s = lax.dot_general(q[:, sl], kb[:, sl], (((1,), (1,)), ((), ())), preferred_element_type=F32)
        e = jnp.exp(s - jnp.max(s, axis=-1, keepdims=True))
        pr = (e / jnp.sum(e, axis=-1, keepdims=True)).astype(BF16)
        heads.append(_dot(pr, vb[:, sl]).astype(BF16))
    o = jnp.concatenate(heads, axis=1)
    y = x + _dot(o, wo_ref[...])
    out_ref[...] = y
    if route:
        lane, gsel, _, i1, i2, _, _ = _route_select(_rmsnorm(y, gf_ref[...]), wrc_ref, br_ref)
        e_lo = LANE_E0 + gsel * EXP_PER_GROUP
        a = jnp.minimum(i1, i2) - e_lo
        b = jnp.maximum(i1, i2) - e_lo
        pair = jnp.where(a == 0, 0, jnp.where(a == 1, 3, 5)) + (b - a - 1)
        bkt_ref[...] = jnp.broadcast_to((gsel * PAIRS_PER_GROUP + pair).astype(F32), bkt_ref.shape)


def _cross_attn(x, k, v, w, *, tm, route):
    B, T, D = x.shape
    tok = pl.BlockSpec((None, tm, D), lambda b, t: (b, t, 0))
    mem = pl.BlockSpec((None, MEM_LEN, D), lambda b, t: (b, 0, 0))
    in_specs = [tok, mem, mem, _const_spec((1, D)), _const_spec((D, D)), _const_spec((D, D))]
    args = [x, k, v, w['g_xa'], w['w_xq'], w['w_xo']]
    out_specs = [tok]
    out_shape = [jax.ShapeDtypeStruct((B, T, D), F32)]
    if route:
        in_specs += [_const_spec((1, D)), _const_spec((D, 2 * ROUTER_LANES)), _const_spec((1, ROUTER_LANES))]
        args += [w['g_ffn'], w['wr_cat'], w['b_r']]
        out_specs.append(pl.BlockSpec((None, tm, ROUTER_LANES), lambda b, t: (b, t, 0)))
        out_shape.append(jax.ShapeDtypeStruct((B, T, ROUTER_LANES), F32))
    res = pl.pallas_call(
        functools.partial(_xattn_kernel, route=route),
        grid=(B, T // tm),
        in_specs=in_specs,
        out_specs=out_specs,
        out_shape=out_shape,
        scratch_shapes=[pltpu.VMEM((MEM_LEN, D), BF16), pltpu.VMEM((MEM_LEN, D), BF16)],
        compiler_params=pltpu.CompilerParams(dimension_semantics=("arbitrary", "arbitrary"),
                                             vmem_limit_bytes=VMEM_LIMIT_BYTES),
        name="cross_attn_route" if route else "cross_attn",
    )(*args)
    return res if route else res[0]


def _route_select(xn, wr_cat_ref, br_ref):
    x_hi = xn.astype(BF16)
    x_lo = (xn - x_hi.astype(F32)).astype(BF16)
    hh_hl = _dot(x_hi, wr_cat_ref[...])
    logits = (hh_hl[:, :ROUTER_LANES] + hh_hl[:, ROUTER_LANES:]
              + _dot(x_lo, wr_cat_ref[:, :ROUTER_LANES]) + br_ref[...])
    lane = lax.broadcasted_iota(jnp.int32, logits.shape, 1)
    neg = jnp.float32(-jnp.inf)
    big = jnp.int32(ROUTER_LANES)

    def first_argmax(vals):
        m = jnp.max(vals, axis=-1, keepdims=True)
        return m, jnp.min(jnp.where(vals == m, lane, big), axis=-1, keepdims=True)

    gl = jnp.where(lane < N_GROUPS, logits, neg)
    gmax, gsel = first_argmax(gl)
    pg = 1.0 / jnp.sum(jnp.exp(gl - gmax), axis=-1, keepdims=True)
    e_lo = LANE_E0 + gsel * EXP_PER_GROUP
    el = jnp.where((lane >= e_lo) & (lane < e_lo + EXP_PER_GROUP), logits, neg)
    v1, i1 = first_argmax(el)
    v2, i2 = first_argmax(jnp.where(lane == i1, neg, el))
    return lane, gsel, pg, i1, i2, v1, v2


def _route(xn, wr_cat_ref, br_ref):
    lane, _, pg, i1, i2, v1, v2 = _route_select(xn, wr_cat_ref, br_ref)
    r = jnp.exp(v2 - v1)
    pe1 = pg / (1.0 + r)
    pe2 = pg * r / (1.0 + r)
    return jnp.where(lane == i1, pe1, 0.0) + jnp.where(lane == i2, pe2, 0.0)


def _moe_dense_kernel(x_ref, g_ref, wrc_ref, br_ref, wg_ref, wu_ref, wd_ref, gf_ref, out_ref, he_buf,
                      *, final_norm):
    x = x_ref[...]
    xn = _rmsnorm(x, g_ref[...])
    comb = _route(xn, wrc_ref, br_ref)
    xb = xn.astype(BF16)
    lane = lax.broadcasted_iota(jnp.int32, comb.shape, 1)
    for e in range(N_EXPERTS):
        c = jnp.sum(jnp.where(lane == e + LANE_E0, comb, 0.0), axis=-1, keepdims=True)
        hg = _dot(xb, wg_ref[e])
        he = c * (hg * _sigmoid(hg) * _dot(xb, wu_ref[e]))
        he_buf[:, e * D_EXPERT:(e + 1) * D_EXPERT] = he.astype(BF16)
    y = x + _dot(he_buf[...], wd_ref[...])
    out_ref[...] = _rmsnorm(y, gf_ref[...]) if final_norm else y


def _moe_dense(x, w, g_final, *, tm, final_norm):
    B, T, D = x.shape
    rows = x.reshape(B * T, D)
    tok = pl.BlockSpec((tm, D), lambda i: (i, 0))
    out = pl.pallas_call(
        functools.partial(_moe_dense_kernel, final_norm=final_norm),
        grid=(B * T // tm,),
        in_specs=[tok, _const_spec((1, D)), _const_spec((D, 2 * ROUTER_LANES)),
                  _const_spec((1, ROUTER_LANES)), _const_spec((N_EXPERTS, D, D_EXPERT)),
                  _const_spec((N_EXPERTS, D, D_EXPERT)), _const_spec((N_EXPERTS * D_EXPERT, D)),
                  _const_spec((1, D))],
        out_specs=tok,
        out_shape=jax.ShapeDtypeStruct((B * T, D), F32),
        scratch_shapes=[pltpu.VMEM((tm, N_EXPERTS * D_EXPERT), BF16)],
        compiler_params=pltpu.CompilerParams(dimension_semantics=("arbitrary",),
                                             vmem_limit_bytes=VMEM_LIMIT_BYTES),
        name="moe_dense",
    )(rows, w['g_ffn'], w['wr_cat'], w['b_r'], w['w_eg'], w['w_eu'],
      w['w_ed'].reshape(N_EXPERTS * D_EXPERT, D), g_final)
    return out.reshape(B, T, D)


def _moe_sorted_kernel(tile_ref, ea_ref, eb_ref, grp_ref, lo_ref, hi_ref, flag_ref,
                       sprev_ref, scur_ref, snext_ref, x_hbm, g_ref, wrh_ref, br_ref,
                       wga_ref, wua_ref, wda_ref, wgb_ref, wub_ref, wdb_ref, gf_ref,
                       y_hbm, xb0, xb1, ob0, ob1, gsem, ssem, *, tmg, n_tiles, final_norm):
    v = pl.program_id(0)
    tile, lo, hi, flags = tile_ref[v], lo_ref[v], hi_ref[v], flag_ref[v]
    xb, ob = (xb0, xb1), (ob0, ob1)
    first = (flags & 1) == 1
    last = (flags & 2) == 2
    has_next = tile < n_tiles - 1
    has_prev = tile > 0
    p_last = (n_tiles - 1) % 2

    def gather_start(idx_ref, k, p, off=0):
        pltpu.make_async_copy(x_hbm.at[pl.ds(idx_ref[0, k] + off, 1)], xb[p].at[pl.ds(k, 1)], gsem.at[p]).start()

    def scatter_start(idx_ref, k, p, off=0):
        pltpu.make_async_copy(ob[p].at[pl.ds(k, 1)], y_hbm.at[pl.ds(idx_ref[0, k] + off, 1)], ssem.at[p]).start()

    def gather_wait(p):
        pltpu.make_async_copy(x_hbm.at[pl.ds(0, tmg)], xb[p], gsem.at[p]).wait()

    def scatter_wait(p):
        pltpu.make_async_copy(ob[p], y_hbm.at[pl.ds(0, tmg)], ssem.at[p]).wait()

    def on_parity(cond, fn):
        for p in (0, 1):
            pl.when(cond & (tile % 2 == p))(functools.partial(fn, p))

    @pl.when(v == 0)
    def _():
        for k in range(tmg):
            gather_start(scur_ref, k, 0)

    def begin_tile(p):
        gather_wait(p)

        @pl.when(tile >= 2)
        def _():
            scatter_wait(p)

        ob[p][...] = jnp.zeros((tmg, D_MODEL), F32)

    on_parity(first, begin_tile)

    def compute(overlap_dma, p):
        pending = []
        if overlap_dma:
            pending += [(gather_start, snext_ref, k) for k in range(tmg)]
            pending += [(scatter_start, sprev_ref, k) for k in range(tmg)]
        n_cols = D_MODEL // D_EXPERT
        per_batch = -(-len(pending) // (2 + 2 * (2 + n_cols)))

        def issue_after(val):
            if not pending:
                return
            bits = pltpu.bitcast(val[0:8, 0:128], jnp.int32)[0, 0]
            zero = lax.shift_right_logical(bits & jnp.int32(0x7FFFFFFF), jnp.int32(31))
            for _ in range(min(per_batch, len(pending))):
                start, idx_ref, k = pending.pop(0)
                start(idx_ref, k, 1 - p, zero)

        x = xb[p][...]
        xn = _rmsnorm(x, g_ref[...]).astype(BF16)
        logits = _dot(xn, wrh_ref[...]) + br_ref[...]
        issue_after(x)
        lane = lax.broadcasted_iota(jnp.int32, logits.shape, 1)
        pick = lambda l: jnp.sum(jnp.where(lane == l, logits, 0.0), axis=-1, keepdims=True)
        gl = jnp.where(lane < N_GROUPS, logits, jnp.float32(-jnp.inf))
        gmax = jnp.max(gl, axis=-1, keepdims=True)
        pg = jnp.exp(pick(grp_ref[v]) - gmax) / jnp.sum(jnp.exp(gl - gmax), axis=-1, keepdims=True)
        va, vb = pick(LANE_E0 + ea_ref[v]), pick(LANE_E0 + eb_ref[v])
        row = lax.broadcasted_iota(jnp.int32, (tmg, 1), 0)
        inside = (row >= lo) & (row < hi)
        prev = logits
        hes = []
        for wt, wg_ref, wu_ref in ((pg / (1.0 + jnp.exp(vb - va)), wga_ref, wua_ref),
                                   (pg / (1.0 + jnp.exp(va - vb)), wgb_ref, wub_ref)):
            hg = _dot(xn, wg_ref[...])
            issue_after(prev)
            hu = _dot(xn, wu_ref[...])
            issue_after(hg)
            prev = hu
            hes.append(jnp.where(inside, wt * (hg * _sigmoid(hg) * hu), 0.0).astype(BF16))
        for c in range(0, D_MODEL, D_EXPERT):
            cols = slice(c, c + D_EXPERT)
            da = _dot(hes[0], wda_ref[:, cols])
            issue_after(prev)
            db = _dot(hes[1], wdb_ref[:, cols])
            issue_after(da)
            prev = db
            ob[p][:, cols] = ob[p][:, cols] + da + db
        issue_after(prev)
        assert not pending

    interior = first & has_next & has_prev
    on_parity(interior, functools.partial(compute, True))

    @pl.when(first & has_next & ~has_prev)
    def _():
        for k in range(tmg):
            gather_start(snext_ref, k, 1)

    @pl.when(first & ~has_next & has_prev)
    def _():
        for k in range(tmg):
            scatter_start(sprev_ref, k, 1 - p_last)

    on_parity((first & ~interior) | (~first & (hi > lo)), functools.partial(compute, False))

    def end_tile(p):
        y = xb[p][...] + ob[p][...]
        ob[p][...] = _rmsnorm(y, gf_ref[...]) if final_norm else y

    on_parity(last, end_tile)

    @pl.when(v == pl.num_programs(0) - 1)
    def _():
        for k in range(tmg):
            scatter_start(scur_ref, k, p_last)
        if n_tiles > 1:
            scatter_wait(1 - p_last)
        scatter_wait(p_last)


def _moe_plan(bucket, n, tmg):
    i32 = jnp.int32
    src = jnp.argsort(bucket, stable=True).astype(i32)
    cnt = jnp.sum((bucket[:, None] == jnp.arange(N_BUCKETS, dtype=i32)[None, :]).astype(i32), axis=0)
    ends = jnp.cumsum(cnt)
    offs = ends - cnt
    nt = n // tmg
    cuts = jnp.sort(jnp.concatenate([jnp.arange(nt, dtype=i32) * tmg, offs[1:]]))
    nxt = jnp.concatenate([cuts[1:], jnp.full((1,), n, i32)])
    tile = jnp.minimum(cuts // tmg, nt - 1)
    bkt = jnp.minimum(jnp.sum((ends[None, :] <= cuts[:, None]).astype(i32), axis=1), N_BUCKETS - 1)
    change = (tile[1:] != tile[:-1]).astype(i32)
    one = jnp.ones((1,), i32)
    flags = jnp.concatenate([one, change]) + 2 * jnp.concatenate([change, one])
    grp = bkt // PAIRS_PER_GROUP
    pair = bkt % PAIRS_PER_GROUP
    ea = grp * EXP_PER_GROUP + jnp.take(jnp.array([0, 0, 0, 1, 1, 2], i32), pair)
    eb = grp * EXP_PER_GROUP + jnp.take(jnp.array([1, 2, 3, 2, 3, 3], i32), pair)
    return src, (tile, ea, eb, grp, cuts - tile * tmg, nxt - tile * tmg, flags)


def _moe_sorted(x, bucket, w, g_final, *, tmg, final_norm):
    B, T, D = x.shape
    n = B * T
    nt = n // tmg
    assert n % tmg == 0
    src, tables = _moe_plan(bucket.reshape(n, ROUTER_LANES)[:, 0].astype(jnp.int32), n, tmg)
    src3 = src.reshape(nt, 1, tmg)
    idx = lambda d: pl.BlockSpec((None, 1, tmg), lambda v, tile, *_: (jnp.clip(tile[v] + d, 0, nt - 1), 0, 0),
                                 memory_space=pltpu.SMEM)
    const = lambda shape: pl.BlockSpec(shape, lambda v, *_: (0,) * len(shape), pipeline_mode=pl.Buffered(1))
    w_a = lambda r, c: pl.BlockSpec((None, r, c), lambda v, tile, ea, eb, *_: (ea[v], 0, 0))
    w_b = lambda r, c: pl.BlockSpec((None, r, c), lambda v, tile, ea, eb, *_: (eb[v], 0, 0))
    any_spec = pl.BlockSpec(memory_space=pl.ANY)
    y = pl.pallas_call(
        functools.partial(_moe_sorted_kernel, tmg=tmg, n_tiles=nt, final_norm=final_norm),
        grid_spec=pltpu.PrefetchScalarGridSpec(
            num_scalar_prefetch=len(tables),
            grid=(tables[0].shape[0],),
            in_specs=[idx(-1), idx(0), idx(1), any_spec, const((1, D)), const((D, ROUTER_LANES)),
                      const((1, ROUTER_LANES)),
                      w_a(D, D_EXPERT), w_a(D, D_EXPERT), w_a(D_EXPERT, D),
                      w_b(D, D_EXPERT), w_b(D, D_EXPERT), w_b(D_EXPERT, D), const((1, D))],
            out_specs=any_spec,
            scratch_shapes=[pltpu.VMEM((tmg, D), F32)] * 4
                           + [pltpu.SemaphoreType.DMA((2,)), pltpu.SemaphoreType.DMA((2,))]),
        out_shape=jax.ShapeDtypeStruct((n, D), F32),
        compiler_params=pltpu.CompilerParams(dimension_semantics=("arbitrary",),
                                             vmem_limit_bytes=VMEM_LIMIT_BYTES, has_side_effects=True),
        name="moe_sorted",
    )(*tables, src3, src3, src3, x.reshape(n, D), w['g_ffn'], w['wr_hi'], w['b_r'],
      w['w_eg'], w['w_eu'], w['w_ed'], w['w_eg'], w['w_eu'], w['w_ed'], g_final)
    return y.reshape(B, T, D)


def _layer_weights(l, g_mix, w_in, b_gate, w_pool, s_pool, g_gv, b_gv, w_s, b_s, w_gmlp_out,
                   w_dw, b_dw, g_cln, b_cln, w_conv_out, w_out, g_xa, g_mem, w_xq, w_xk, w_xv, w_xo,
                   g_ffn, w_rg, b_rg, w_re, b_re, w_eg, w_eu, w_ed):
    row = lambda a: a[l].reshape(1, -1)
    wp = jnp.zeros((POOL_WIDTH, D_MODEL), F32)
    gout = D_MODEL // len(POOL_WINDOWS)
    for g in range(len(POOL_WINDOWS)):
        wp = wp.at[g * POOL_GDIM:(g + 1) * POOL_GDIM, g * gout:(g + 1) * gout].set(w_pool[l, g])
    wr = jnp.zeros((D_MODEL, ROUTER_LANES), F32)
    wr = wr.at[:, :N_GROUPS].set(w_rg[l]).at[:, LANE_E0:LANE_E0 + N_EXPERTS].set(w_re[l])
    wr_hi = wr.astype(BF16)
    br = jnp.zeros((1, ROUTER_LANES), F32)
    br = br.at[0, :N_GROUPS].set(b_rg[l]).at[0, LANE_E0:LANE_E0 + N_EXPERTS].set(b_re[l])
    return {
        'g_mix': row(g_mix), 'w_in': w_in[l].astype(BF16), 'b_gate': b_gate[l],
        'w_pool_bd': wp.astype(BF16), 's_pool': row(s_pool), 'g_gv': row(g_gv), 'b_gv': row(b_gv),
        'w_s': w_s[l], 'b_s_t': jnp.transpose(b_s[l]), 'w_gmlp_out': w_gmlp_out[l].astype(BF16),
        'w_dw': w_dw[l], 'b_dw': row(b_dw), 'g_cln': row(g_cln), 'b_cln': row(b_cln),
        'w_conv_out': w_conv_out[l].astype(BF16), 'w_out': w_out[l].astype(BF16),
        'g_xa': row(g_xa), 'g_mem': row(g_mem), 'w_xq': (w_xq[l] * (XA_HDIM ** -0.5)).astype(BF16),
        'w_xk': w_xk[l].astype(BF16), 'w_xv': w_xv[l].astype(BF16), 'w_xo': w_xo[l].astype(BF16),
        'g_ffn': row(g_ffn), 'wr_hi': wr_hi, 'b_r': br,
        'wr_cat': jnp.concatenate([wr_hi, (wr - wr_hi.astype(F32)).astype(BF16)], axis=1),
        'w_eg': w_eg[l].astype(BF16), 'w_eu': w_eu[l].astype(BF16), 'w_ed': w_ed[l].astype(BF16),
    }


def _attn_moe(x, k, v, w, g_final, *, tm, final_norm):
    n = x.shape[0] * x.shape[1]
    if n >= MOE_SORTED_MIN_ROWS:
        x, bucket = _cross_attn(x, k, v, w, tm=tm, route=True)
        return _moe_sorted(x, bucket, w, g_final, tmg=MOE_SORTED_TILE, final_norm=final_norm)
    x = _cross_attn(x, k, v, w, tm=tm, route=False)
    return _moe_dense(x, w, g_final, tm=_token_tile(n), final_norm=final_norm)


def _pad_hist(h, rows):
    return jnp.pad(h, ((0, 0), (rows - h.shape[1], 0), (0, 0)))


def _token_tile(t):
    return min(1024, t)


def kernel(x_prompt, x_sample, mem_prompt, cache_pool, cache_conv, cache_mem_k, cache_mem_v, g_mix, w_in, b_gate, w_pool, s_pool, g_gv, b_gv, w_s, b_s, w_gmlp_out, w_dw, b_dw, g_cln, b_cln, w_conv_out, w_out, g_xa, g_mem, w_xq, w_xk, w_xv, w_xo, g_ffn, w_rg, b_rg, w_re, b_re, w_eg, w_eu, w_ed, g_final):
    depth = w_in.shape[0]
    bp, tp, _ = x_prompt.shape
    bs, ts, _ = x_sample.shape
    tmp, tms = _token_tile(tp), _token_tile(ts)
    gf = g_final.reshape(1, -1)
    xp, xs = x_prompt, x_sample
    pool0 = jnp.zeros((bp, POOL_HIST_PAD, POOL_WIDTH), F32)
    conv0 = jnp.zeros((bp, CONV_HIST_PAD, CONV_WIDTH), F32)
    outs = {k: [] for k in ('pp', 'pc', 'pk', 'pv', 'sp', 'sc', 'sv')}
    for l in range(depth):
        w = _layer_weights(l, g_mix, w_in, b_gate, w_pool, s_pool, g_gv, b_gv, w_s, b_s, w_gmlp_out,
                           w_dw, b_dw, g_cln, b_cln, w_conv_out, w_out, g_xa, g_mem, w_xq, w_xk, w_xv,
                           w_xo, g_ffn, w_rg, b_rg, w_re, b_re, w_eg, w_eu, w_ed)
        last = l == depth - 1
        xp, ph, ch = _mixer(xp, pool0, conv0, w, tm=tmp, pos0=0, emit_vn=False)
        mk, mv = _memory_kv(mem_prompt, w, tm=_token_tile(bp * MEM_LEN))
        xp = _attn_moe(xp, mk, mv, w, gf, tm=tmp, final_norm=last)
        outs['pp'].append(ph[:, POOL_HIST_PAD - POOL_HIST:])
        outs['pc'].append(ch[:, CONV_HIST_PAD - CONV_HIST:])
        outs['pk'].append(mk.reshape(bp, MEM_LEN, XA_HEADS, XA_HDIM))
        outs['pv'].append(mv.reshape(bp, MEM_LEN, XA_HEADS, XA_HDIM))
        xs, sh, sc, vn = _mixer(xs, _pad_hist(cache_pool[l], POOL_HIST_PAD),
                                _pad_hist(cache_conv[l], CONV_HIST_PAD), w,
                                tm=tms, pos0=PAST_LEN, emit_vn=True)
        xs = _attn_moe(xs, cache_mem_k[l].reshape(bs, MEM_LEN, D_MODEL),
                       cache_mem_v[l].reshape(bs, MEM_LEN, D_MODEL), w, gf, tm=tms, final_norm=last)
        outs['sp'].append(sh[:, POOL_HIST_PAD - POOL_HIST:])
        outs['sc'].append(sc[:, CONV_HIST_PAD - CONV_HIST:])
        outs['sv'].append(vn)
    st = lambda k: jnp.stack(outs[k], axis=0)
    return (xp, xs, st('pp'), st('pc'), st('pk'), st('pv'), st('sp'), st('sc'), st('sv'))
```

```python
import functools

import jax
import jax.numpy as jnp
from jax import lax
from jax.experimental import pallas as pl
from jax.experimental.pallas import tpu as pltpu

F32 = jnp.float32
BF16 = jnp.bfloat16

EPS = 1e-6
D_MODEL = 1024
PAST_LEN = 4096
CHUNK = 64
POOL_WIDTH = 256
POOL_WINDOWS = (2, 4, 8, 16)
POOL_GDIM = 64
POOL_HIST = 15
POOL_HIST_PAD = 32
GMLP_WIDTH = 512
GMLP_HEADS = 4
GMLP_HDIM = 128
GMLP_CHUNK = 128
CONV_WIDTH = 256
CONV_K = 31
CONV_HIST = 30
CONV_HIST_PAD = 32
SUBLANES = 8
CONV_ROWS = 64
N_BRANCH = 3
O_POOL = 0
O_GMLP = POOL_WIDTH
O_CONV = O_GMLP + 2 * GMLP_WIDTH
O_GATE = O_CONV + 2 * CONV_WIDTH
IN_COLS = O_GATE + N_BRANCH * D_MODEL
MEM_LEN = 256
XA_HEADS = 4
XA_HDIM = 256
N_GROUPS = 4
EXP_PER_GROUP = 4
N_EXPERTS = 16
D_EXPERT = 256
ROUTER_LANES = 128
LANE_E0 = N_GROUPS
PAIRS_PER_GROUP = 6
N_BUCKETS = N_GROUPS * PAIRS_PER_GROUP
MOE_SORTED_TILE = 256
MOE_SORTED_MIN_ROWS = 1024

VMEM_LIMIT_BYTES = 56 * 1024 * 1024


def _dot(a, b):
    return jnp.dot(a, b, preferred_element_type=F32)


def _rmsnorm(x, g):
    return x * lax.rsqrt(jnp.mean(x * x, axis=-1, keepdims=True) + EPS) * g


def _layernorm(x, g, b):
    mu = jnp.mean(x, axis=-1, keepdims=True)
    xc = x - mu
    var = jnp.mean(xc * xc, axis=-1, keepdims=True)
    return xc * lax.rsqrt(var + EPS) * g + b


def _sigmoid(x):
    return 1.0 / (1.0 + jnp.exp(-x))


def _const_spec(shape):
    nd = len(shape)
    return pl.BlockSpec(shape, lambda *_: (0,) * nd, pipeline_mode=pl.Buffered(1))


def _mixer_kernel(x_ref, ph_ref, ch_ref, g_ref, win_ref, bg_ref, wp_ref, sp_ref, ggv_ref, bgv_ref,
                  ws_ref, bs_ref, wgo_ref, wdw_ref, bdw_ref, gcl_ref, bcl_ref, wco_ref, wo_ref,
                  *rest, tm, pos0, emit_vn):
    if emit_vn:
        out_ref, pt_ref, at_ref, vn_ref, pbuf, abuf, sbuf, s2buf, s4buf, s8buf = rest
    else:
        out_ref, pt_ref, at_ref, pbuf, abuf, sbuf, s2buf, s4buf, s8buf = rest
        vn_ref = None
    t = pl.program_id(1)

    @pl.when(t == 0)
    def _():
        pbuf[0:POOL_HIST_PAD, :] = ph_ref[...]
        abuf[0:CONV_HIST_PAD, :] = ch_ref[...]

    cr = min(GMLP_CHUNK, tm)
    bi = lax.broadcasted_iota(jnp.int32, (cr, cr), 0) // CHUNK
    bj = lax.broadcasted_iota(jnp.int32, (cr, cr), 1) // CHUNK
    ws = [jnp.where(bi >= bj, ws_ref[h, 0:cr, 0:cr], 0.0).astype(BF16) for h in range(GMLP_HEADS)]
    lane = lax.broadcasted_iota(jnp.int32, (tm, 128), 1)

    x = x_ref[...]
    xn = _rmsnorm(x, g_ref[...]).astype(BF16)

    p = _dot(xn, win_ref[:, O_POOL:O_POOL + POOL_WIDTH])
    pbuf[POOL_HIST_PAD:POOL_HIST_PAD + tm, :] = p
    pos = (pos0 + t * tm + lax.broadcasted_iota(jnp.int32, (tm, 1), 0) + 1).astype(F32)
    lo2, hi = SUBLANES, POOL_HIST_PAD + tm
    s2buf[lo2:hi, :] = pbuf[lo2:hi, :] + pbuf[lo2 - 1:hi - 1, :]
    s4buf[lo2 + 8:hi, :] = s2buf[lo2 + 8:hi, :] + s2buf[lo2 + 6:hi - 2, :]
    s8buf[lo2 + 16:hi, :] = s4buf[lo2 + 16:hi, 128:256] + s4buf[lo2 + 12:hi - 4, 128:256]
    r0 = POOL_HIST_PAD
    s8 = s8buf[r0:hi, :]
    s16 = s8 + s8buf[r0 - 8:hi - 8, :]
    w2, w4, w8, w16 = (jnp.minimum(float(w), pos) for w in POOL_WINDOWS)
    means = [jnp.where(lane < POOL_GDIM, s2buf[r0:hi, 0:128] / w2, s4buf[r0:hi, 0:128] / w4),
             jnp.where(lane < POOL_GDIM, s8 / w8, s16 / w16)]
    d = (jnp.concatenate(means, axis=1) - p).astype(BF16)
    ya = _dot(d, wp_ref[...]) * sp_ref[...]
    merged = _sigmoid(_dot(xn, win_ref[:, O_GATE:O_GATE + D_MODEL]) + bg_ref[0:1, :]) * ya

    z = jax.nn.gelu(_dot(xn, win_ref[:, O_GMLP:O_GMLP + 2 * GMLP_WIDTH]))
    u = z[:, :GMLP_WIDTH]
    vn = _layernorm(z[:, GMLP_WIDTH:], ggv_ref[...], bgv_ref[...])
    if emit_vn:
        vn_ref[...] = vn
    vb = vn.astype(BF16)
    chunks = [jnp.concatenate(
        [_dot(ws[h], vb[c0:c0 + cr, h * GMLP_HDIM:(h + 1) * GMLP_HDIM]) + bs_ref[0:cr, h:h + 1]
         for h in range(GMLP_HEADS)], axis=1) for c0 in range(0, tm, cr)]
    mixed = chunks[0] if len(chunks) == 1 else jnp.concatenate(chunks, axis=0)
    yb = _dot((u * mixed).astype(BF16), wgo_ref[...])
    merged = merged + _sigmoid(_dot(xn, win_ref[:, O_GATE + D_MODEL:O_GATE + 2 * D_MODEL])
                               + bg_ref[1:2, :]) * yb

    cc = _dot(xn, win_ref[:, O_CONV:O_CONV + 2 * CONV_WIDTH])
    a = cc[:, :CONV_WIDTH] * _sigmoid(cc[:, CONV_WIDTH:])
    abuf[CONV_HIST_PAD:CONV_HIST_PAD + tm, :] = a
    span = tm + CONV_HIST_PAD - SUBLANES
    for r in range(1, SUBLANES):
        sbuf[r - 1, :, :] = abuf[r:r + span, :]

    def tap(k, r0, n):
        q, r = divmod(CONV_HIST_PAD - CONV_HIST + k, SUBLANES)
        lo = SUBLANES * q + r0
        return abuf[lo:lo + n, :] if r == 0 else sbuf[r - 1, lo:lo + n, :]

    rc = min(CONV_ROWS, tm)
    hs = []
    for r0 in range(0, tm, rc):
        acc = tap(0, r0, rc) * wdw_ref[0:1, :]
        for k in range(1, CONV_K):
            acc = acc + tap(k, r0, rc) * wdw_ref[k:k + 1, :]
        hln = _layernorm(acc + bdw_ref[...], gcl_ref[...], bcl_ref[...])
        hs.append((hln * _sigmoid(hln)).astype(BF16))
    yc = _dot(hs[0] if len(hs) == 1 else jnp.concatenate(hs, axis=0), wco_ref[...])
    merged = merged + _sigmoid(_dot(xn, win_ref[:, O_GATE + 2 * D_MODEL:O_GATE + 3 * D_MODEL])
                               + bg_ref[2:3, :]) * yc

    out_ref[...] = x + _dot(merged.astype(BF16), wo_ref[...])

    p_tail = pbuf[tm:tm + POOL_HIST_PAD, :]
    a_tail = abuf[tm:tm + CONV_HIST_PAD, :]
    pbuf[0:POOL_HIST_PAD, :] = p_tail
    abuf[0:CONV_HIST_PAD, :] = a_tail
    pt_ref[...] = p_tail
    at_ref[...] = a_tail


def _mixer(x, pool_hist, conv_hist, w, *, tm, pos0, emit_vn):
    B, T, D = x.shape
    assert T % tm == 0 and tm >= CONV_HIST_PAD and (tm % GMLP_CHUNK == 0 or tm == T)
    tok = pl.BlockSpec((None, tm, D), lambda b, t: (b, t, 0))
    per_b = lambda r, c: pl.BlockSpec((None, r, c), lambda b, t: (b, 0, 0))
    weights = [w['g_mix'], w['w_in'], w['b_gate'], w['w_pool_bd'], w['s_pool'], w['g_gv'], w['b_gv'],
               w['w_s'], w['b_s_t'], w['w_gmlp_out'], w['w_dw'], w['b_dw'], w['g_cln'], w['b_cln'],
               w['w_conv_out'], w['w_out']]
    out_shape = [jax.ShapeDtypeStruct((B, T, D), F32),
                 jax.ShapeDtypeStruct((B, POOL_HIST_PAD, POOL_WIDTH), F32),
                 jax.ShapeDtypeStruct((B, CONV_HIST_PAD, CONV_WIDTH), F32)]
    out_specs = [tok, per_b(POOL_HIST_PAD, POOL_WIDTH), per_b(CONV_HIST_PAD, CONV_WIDTH)]
    if emit_vn:
        out_shape.append(jax.ShapeDtypeStruct((B, T, GMLP_WIDTH), F32))
        out_specs.append(pl.BlockSpec((None, tm, GMLP_WIDTH), lambda b, t: (b, t, 0)))
    return pl.pallas_call(
        functools.partial(_mixer_kernel, tm=tm, pos0=pos0, emit_vn=emit_vn),
        grid=(B, T // tm),
        in_specs=[tok, per_b(POOL_HIST_PAD, POOL_WIDTH), per_b(CONV_HIST_PAD, CONV_WIDTH)]
                 + [_const_spec(a.shape) for a in weights],
        out_specs=out_specs,
        out_shape=out_shape,
        scratch_shapes=[pltpu.VMEM((POOL_HIST_PAD + tm, POOL_WIDTH), F32),
                        pltpu.VMEM((CONV_HIST_PAD + tm, CONV_WIDTH), F32),
                        pltpu.VMEM((SUBLANES - 1, CONV_HIST_PAD + tm - SUBLANES, CONV_WIDTH), F32),
                        pltpu.VMEM((POOL_HIST_PAD + tm, POOL_WIDTH), F32),
                        pltpu.VMEM((POOL_HIST_PAD + tm, POOL_WIDTH), F32),
                        pltpu.VMEM((POOL_HIST_PAD + tm, POOL_WIDTH // 2), F32)],
        compiler_params=pltpu.CompilerParams(dimension_semantics=("arbitrary", "arbitrary"),
                                             vmem_limit_bytes=VMEM_LIMIT_BYTES),
        name="mixer",
    )(x, pool_hist, conv_hist, *weights)


def _kv_kernel(m_ref, g_ref, wk_ref, wv_ref, k_ref, v_ref):
    mn = _rmsnorm(m_ref[...], g_ref[...]).astype(BF16)
    k_ref[...] = _dot(mn, wk_ref[...])
    v_ref[...] = _dot(mn, wv_ref[...])


def _memory_kv(mem, w, *, tm):
    B, M, D = mem.shape
    rows = mem.reshape(B * M, D)
    blk = pl.BlockSpec((tm, D), lambda i: (i, 0))
    k, v = pl.pallas_call(
        _kv_kernel,
        grid=(B * M // tm,),
        in_specs=[blk, _const_spec((1, D)), _const_spec((D, D)), _const_spec((D, D))],
        out_specs=[blk, blk],
        out_shape=[jax.ShapeDtypeStruct((B * M, D), F32)] * 2,
        compiler_params=pltpu.CompilerParams(dimension_semantics=("arbitrary",),
                                             vmem_limit_bytes=VMEM_LIMIT_BYTES),
        name="memory_kv",
    )(rows, w['g_mem'], w['w_xk'], w['w_xv'])
    return k.reshape(B, M, D), v.reshape(B, M, D)


def _xattn_kernel(x_ref, k_ref, v_ref, g_ref, wq_ref, wo_ref, *rest, route):
    if route:
        gf_ref, wrc_ref, br_ref, out_ref, bkt_ref, kb, vb = rest
    else:
        out_ref, kb, vb = rest

    @pl.when(pl.program_id(1) == 0)
    def _():
        kb[...] = k_ref[...].astype(BF16)
        vb[...] = v_ref[...].astype(BF16)

    x = x_ref[...]
    xn = _rmsnorm(x, g_ref[...]).astype(BF16)
    q = _dot(xn, wq_ref[...]).astype(BF16)
    heads = []
    for h in range(XA_HEADS):
        sl = slice(h * XA_HDIM, (h + 1) * XA_HDIM)
        s = lax.dot_general(q[:, sl], kb[:, sl], (((1,), (1,)), ((), ())), preferred_element_type=F32)
        e = jnp.exp(s - jnp.max(s, axis=-1, keepdims=True))
        pr = (e / jnp.sum(e, axis=-1, keepdims=True)).astype(BF16)
        heads.append(_dot(pr, vb[:, sl]).astype(BF16))
    o = jnp.concatenate(heads, axis=1)
    y = x + _dot(o, wo_ref[...])
    out_ref[...] = y
    if route:
        lane, gsel, _, i1, i2, _, _ = _route_select(_rmsnorm(y, gf_ref[...]), wrc_ref, br_ref)
        e_lo = LANE_E0 + gsel * EXP_PER_GROUP
        a = jnp.minimum(i1, i2) - e_lo
        b = jnp.maximum(i1, i2) - e_lo
        pair = jnp.where(a == 0, 0, jnp.where(a == 1, 3, 5)) + (b - a - 1)
        bkt_ref[...] = jnp.broadcast_to((gsel * PAIRS_PER_GROUP + pair).astype(F32), bkt_ref.shape)


def _cross_attn(x, k, v, w, *, tm, route):
    B, T, D = x.shape
    tok = pl.BlockSpec((None, tm, D), lambda b, t: (b, t, 0))
    mem = pl.BlockSpec((None, MEM_LEN, D), lambda b, t: (b, 0, 0))
    in_specs = [tok, mem, mem, _const_spec((1, D)), _const_spec((D, D)), _const_spec((D, D))]
    args = [x, k, v, w['g_xa'], w['w_xq'], w['w_xo']]
    out_specs = [tok]
    out_shape = [jax.ShapeDtypeStruct((B, T, D), F32)]
    if route:
        in_specs += [_const_spec((1, D)), _const_spec((D, 2 * ROUTER_LANES)), _const_spec((1, ROUTER_LANES))]
        args += [w['g_ffn'], w['wr_cat'], w['b_r']]
        out_specs.append(pl.BlockSpec((None, tm, ROUTER_LANES), lambda b, t: (b, t, 0)))
        out_shape.append(jax.ShapeDtypeStruct((B, T, ROUTER_LANES), F32))
    res = pl.pallas_call(
        functools.partial(_xattn_kernel, route=route),
        grid=(B, T // tm),
        in_specs=in_specs,
        out_specs=out_specs,
        out_shape=out_shape,
        scratch_shapes=[pltpu.VMEM((MEM_LEN, D), BF16), pltpu.VMEM((MEM_LEN, D), BF16)],
        compiler_params=pltpu.CompilerParams(dimension_semantics=("arbitrary", "arbitrary"),
                                             vmem_limit_bytes=VMEM_LIMIT_BYTES),
        name="cross_attn_route" if route else "cross_attn",
    )(*args)
    return res if route else res[0]


def _route_select(xn, wr_cat_ref, br_ref):
    x_hi = xn.astype(BF16)
    x_lo = (xn - x_hi.astype(F32)).astype(BF16)
    hh_hl = _dot(x_hi, wr_cat_ref[...])
    logits = (hh_hl[:, :ROUTER_LANES] + hh_hl[:, ROUTER_LANES:]
              + _dot(x_lo, wr_cat_ref[:, :ROUTER_LANES]) + br_ref[...])
    lane = lax.broadcasted_iota(jnp.int32, logits.shape, 1)
    neg = jnp.float32(-jnp.inf)
    big = jnp.int32(ROUTER_LANES)

    def first_argmax(vals):
        m = jnp.max(vals, axis=-1, keepdims=True)
        return m, jnp.min(jnp.where(vals == m, lane, big), axis=-1, keepdims=True)

    gl = jnp.where(lane < N_GROUPS, logits, neg)
    gmax, gsel = first_argmax(gl)
    pg = 1.0 / jnp.sum(jnp.exp(gl - gmax), axis=-1, keepdims=True)
    e_lo = LANE_E0 + gsel * EXP_PER_GROUP
    el = jnp.where((lane >= e_lo) & (lane < e_lo + EXP_PER_GROUP), logits, neg)
    v1, i1 = first_argmax(el)
    v2, i2 = first_argmax(jnp.where(lane == i1, neg, el))
    return lane, gsel, pg, i1, i2, v1, v2


def _route(xn, wr_cat_ref, br_ref):
    lane, _, pg, i1, i2, v1, v2 = _route_select(xn, wr_cat_ref, br_ref)
    r = jnp.exp(v2 - v1)
    pe1 = pg / (1.0 + r)
    pe2 = pg * r / (1.0 + r)
    return jnp.where(lane == i1, pe1, 0.0) + jnp.where(lane == i2, pe2, 0.0)


def _moe_dense_kernel(x_ref, g_ref, wrc_ref, br_ref, wg_ref, wu_ref, wd_ref, gf_ref, out_ref, he_buf,
                      *, final_norm):
    x = x_ref[...]
    xn = _rmsnorm(x, g_ref[...])
    comb = _route(xn, wrc_ref, br_ref)
    xb = xn.astype(BF16)
    lane = lax.broadcasted_iota(jnp.int32, comb.shape, 1)
    for e in range(N_EXPERTS):
        c = jnp.sum(jnp.where(lane == e + LANE_E0, comb, 0.0), axis=-1, keepdims=True)
        hg = _dot(xb, wg_ref[e])
        he = c * (hg * _sigmoid(hg) * _dot(xb, wu_ref[e]))
        he_buf[:, e * D_EXPERT:(e + 1) * D_EXPERT] = he.astype(BF16)
    y = x + _dot(he_buf[...], wd_ref[...])
    out_ref[...] = _rmsnorm(y, gf_ref[...]) if final_norm else y


def _moe_dense(x, w, g_final, *, tm, final_norm):
    B, T, D = x.shape
    rows = x.reshape(B * T, D)
    tok = pl.BlockSpec((tm, D), lambda i: (i, 0))
    out = pl.pallas_call(
        functools.partial(_moe_dense_kernel, final_norm=final_norm),
        grid=(B * T // tm,),
        in_specs=[tok, _const_spec((1, D)), _const_spec((D, 2 * ROUTER_LANES)),
                  _const_spec((1, ROUTER_LANES)), _const_spec((N_EXPERTS, D, D_EXPERT)),
                  _const_spec((N_EXPERTS, D, D_EXPERT)), _const_spec((N_EXPERTS * D_EXPERT, D)),
                  _const_spec((1, D))],
        out_specs=tok,
        out_shape=jax.ShapeDtypeStruct((B * T, D), F32),
        scratch_shapes=[pltpu.VMEM((tm, N_EXPERTS * D_EXPERT), BF16)],
        compiler_params=pltpu.CompilerParams(dimension_semantics=("arbitrary",),
                                             vmem_limit_bytes=VMEM_LIMIT_BYTES),
        name="moe_dense",
    )(rows, w['g_ffn'], w['wr_cat'], w['b_r'], w['w_eg'], w['w_eu'],
      w['w_ed'].reshape(N_EXPERTS * D_EXPERT, D), g_final)
    return out.reshape(B, T, D)


def _moe_sorted_kernel(tile_ref, ea_ref, eb_ref, grp_ref, lo_ref, hi_ref, flag_ref,
                       sprev_ref, scur_ref, snext_ref, x_hbm, g_ref, wrh_ref, br_ref,
                       wga_ref, wua_ref, wda_ref, wgb_ref, wub_ref, wdb_ref, gf_ref,
                       y_hbm, xb0, xb1, ob0, ob1, gsem, ssem, *, tmg, n_tiles, final_norm):
    v = pl.program_id(0)
    tile, lo, hi, flags = tile_ref[v], lo_ref[v], hi_ref[v], flag_ref[v]
    xb, ob = (xb0, xb1), (ob0, ob1)
    first = (flags & 1) == 1
    last = (flags & 2) == 2
    has_next = tile < n_tiles - 1
    has_prev = tile > 0
    p_last = (n_tiles - 1) % 2

    def gather_start(idx_ref, k, p, off=0):
        pltpu.make_async_copy(x_hbm.at[pl.ds(idx_ref[0, k] + off, 1)], xb[p].at[pl.ds(k, 1)], gsem.at[p]).start()

    def scatter_start(idx_ref, k, p, off=0):
        pltpu.make_async_copy(ob[p].at[pl.ds(k, 1)], y_hbm.at[pl.ds(idx_ref[0, k] + off, 1)], ssem.at[p]).start()

    def gather_wait(p):
        pltpu.make_async_copy(x_hbm.at[pl.ds(0, tmg)], xb[p], gsem.at[p]).wait()

    def scatter_wait(p):
        pltpu.make_async_copy(ob[p], y_hbm.at[pl.ds(0, tmg)], ssem.at[p]).wait()

    def on_parity(cond, fn):
        for p in (0, 1):
            pl.when(cond & (tile % 2 == p))(functools.partial(fn, p))

    @pl.when(v == 0)
    def _():
        for k in range(tmg):
            gather_start(scur_ref, k, 0)

    def wait_ob_free(p):
        @pl.when(tile >= 2)
        def _():
            scatter_wait(p)

    interior = first & has_next & has_prev

    def begin_tile(clear, p):
        gather_wait(p)
        if clear:
            wait_ob_free(p)
            ob[p][...] = jnp.zeros((tmg, D_MODEL), F32)

    on_parity(interior, functools.partial(begin_tile, False))
    on_parity(first & ~interior, functools.partial(begin_tile, True))

    def compute(overlap_dma, p):
        pending = []
        if overlap_dma:
            pending += [(gather_start, snext_ref, k) for k in range(tmg)]
            pending += [(scatter_start, sprev_ref, k) for k in range(tmg)]
        n_cols = D_MODEL // D_EXPERT
        per_batch = -(-len(pending) // (2 + 2 * (2 + n_cols)))

        def issue_after(val):
            if not pending:
                return
            bits = pltpu.bitcast(val[0:8, 0:128], jnp.int32)[0, 0]
            zero = lax.shift_right_logical(bits & jnp.int32(0x7FFFFFFF), jnp.int32(31))
            for _ in range(min(per_batch, len(pending))):
                start, idx_ref, k = pending.pop(0)
                start(idx_ref, k, 1 - p, zero)

        x = xb[p][...]
        xn = _rmsnorm(x, g_ref[...]).astype(BF16)
        logits = _dot(xn, wrh_ref[...]) + br_ref[...]
        issue_after(x)
        lane = lax.broadcasted_iota(jnp.int32, logits.shape, 1)
        pick = lambda l: jnp.sum(jnp.where(lane == l, logits, 0.0), axis=-1, keepdims=True)
        gl = jnp.where(lane < N_GROUPS, logits, jnp.float32(-jnp.inf))
        gmax = jnp.max(gl, axis=-1, keepdims=True)
        pg = jnp.exp(pick(grp_ref[v]) - gmax) / jnp.sum(jnp.exp(gl - gmax), axis=-1, keepdims=True)
        va, vb = pick(LANE_E0 + ea_ref[v]), pick(LANE_E0 + eb_ref[v])
        row = lax.broadcasted_iota(jnp.int32, (tmg, 1), 0)
        inside = (row >= lo) & (row < hi)
        prev = logits
        hes = []
        for wt, wg_ref, wu_ref in ((pg / (1.0 + jnp.exp(vb - va)), wga_ref, wua_ref),
                                   (pg / (1.0 + jnp.exp(va - vb)), wgb_ref, wub_ref)):
            hg = _dot(xn, wg_ref[...])
            issue_after(prev)
            hu = _dot(xn, wu_ref[...])
            issue_after(hg)
            prev = hu
            hes.append(jnp.where(inside, wt * (hg * _sigmoid(hg) * hu), 0.0).astype(BF16))
        for c in range(0, D_MODEL, D_EXPERT):
            cols = slice(c, c + D_EXPERT)
            da = _dot(hes[0], wda_ref[:, cols])
            issue_after(prev)
            db = _dot(hes[1], wdb_ref[:, cols])
            issue_after(da)
            prev = db
            if overlap_dma:
                if c == 0:
                    wait_ob_free(p)
                ob[p][:, cols] = da + db
            else:
                ob[p][:, cols] = ob[p][:, cols] + da + db
        issue_after(prev)
        assert not pending

    on_parity(interior, functools.partial(compute, True))

    @pl.when(first & has_next & ~has_prev)
    def _():
        for k in range(tmg):
            gather_start(snext_ref, k, 1)

    @pl.when(first & ~has_next & has_prev)
    def _():
        for k in range(tmg):
            scatter_start(sprev_ref, k, 1 - p_last)

    on_parity((first & ~interior) | (~first & (hi > lo)), functools.partial(compute, False))

    def end_tile(p):
        y = xb[p][...] + ob[p][...]
        ob[p][...] = _rmsnorm(y, gf_ref[...]) if final_norm else y

    on_parity(last, end_tile)

    @pl.when(v == pl.num_programs(0) - 1)
    def _():
        for k in range(tmg):
            scatter_start(scur_ref, k, p_last)
        if n_tiles > 1:
            scatter_wait(1 - p_last)
        scatter_wait(p_last)


def _moe_plan(bucket, n, tmg):
    i32 = jnp.int32
    src = jnp.argsort(bucket, stable=True).astype(i32)
    cnt = jnp.sum((bucket[:, None] == jnp.arange(N_BUCKETS, dtype=i32)[None, :]).astype(i32), axis=0)
    ends = jnp.cumsum(cnt)
    offs = ends - cnt
    nt = n // tmg
    cuts = jnp.sort(jnp.concatenate([jnp.arange(nt, dtype=i32) * tmg, offs[1:]]))
    nxt = jnp.concatenate([cuts[1:], jnp.full((1,), n, i32)])
    tile = jnp.minimum(cuts // tmg, nt - 1)
    bkt = jnp.minimum(jnp.sum((ends[None, :] <= cuts[:, None]).astype(i32), axis=1), N_BUCKETS - 1)
    change = (tile[1:] != tile[:-1]).astype(i32)
    one = jnp.ones((1,), i32)
    flags = jnp.concatenate([one, change]) + 2 * jnp.concatenate([change, one])
    grp = bkt // PAIRS_PER_GROUP
    pair = bkt % PAIRS_PER_GROUP
    ea = grp * EXP_PER_GROUP + jnp.take(jnp.array([0, 0, 0, 1, 1, 2], i32), pair)
    eb = grp * EXP_PER_GROUP + jnp.take(jnp.array([1, 2, 3, 2, 3, 3], i32), pair)
    return src, (tile, ea, eb, grp, cuts - tile * tmg, nxt - tile * tmg, flags)


def _moe_sorted(x, bucket, w, g_final, *, tmg, final_norm):
    B, T, D = x.shape
    n = B * T
    nt = n // tmg
    assert n % tmg == 0
    src, tables = _moe_plan(bucket.reshape(n, ROUTER_LANES)[:, 0].astype(jnp.int32), n, tmg)
    src3 = src.reshape(nt, 1, tmg)
    idx = lambda d: pl.BlockSpec((None, 1, tmg), lambda v, tile, *_: (jnp.clip(tile[v] + d, 0, nt - 1), 0, 0),
                                 memory_space=pltpu.SMEM)
    const = lambda shape: pl.BlockSpec(shape, lambda v, *_: (0,) * len(shape), pipeline_mode=pl.Buffered(1))
    w_a = lambda r, c: pl.BlockSpec((None, r, c), lambda v, tile, ea, eb, *_: (ea[v], 0, 0))
    w_b = lambda r, c: pl.BlockSpec((None, r, c), lambda v, tile, ea, eb, *_: (eb[v], 0, 0))
    any_spec = pl.BlockSpec(memory_space=pl.ANY)
    y = pl.pallas_call(
        functools.partial(_moe_sorted_kernel, tmg=tmg, n_tiles=nt, final_norm=final_norm),
        grid_spec=pltpu.PrefetchScalarGridSpec(
            num_scalar_prefetch=len(tables),
            grid=(tables[0].shape[0],),
            in_specs=[idx(-1), idx(0), idx(1), any_spec, const((1, D)), const((D, ROUTER_LANES)),
                      const((1, ROUTER_LANES)),
                      w_a(D, D_EXPERT), w_a(D, D_EXPERT), w_a(D_EXPERT, D),
                      w_b(D, D_EXPERT), w_b(D, D_EXPERT), w_b(D_EXPERT, D), const((1, D))],
            out_specs=any_spec,
            scratch_shapes=[pltpu.VMEM((tmg, D), F32)] * 4
                           + [pltpu.SemaphoreType.DMA((2,)), pltpu.SemaphoreType.DMA((2,))]),
        out_shape=jax.ShapeDtypeStruct((n, D), F32),
        compiler_params=pltpu.CompilerParams(dimension_semantics=("arbitrary",),
                                             vmem_limit_bytes=VMEM_LIMIT_BYTES, has_side_effects=True),
        name="moe_sorted",
    )(*tables, src3, src3, src3, x.reshape(n, D), w['g_ffn'], w['wr_hi'], w['b_r'],
      w['w_eg'], w['w_eu'], w['w_ed'], w['w_eg'], w['w_eu'], w['w_ed'], g_final)
    return y.reshape(B, T, D)


def _layer_weights(l, g_mix, w_in, b_gate, w_pool, s_pool, g_gv, b_gv, w_s, b_s, w_gmlp_out,
                   w_dw, b_dw, g_cln, b_cln, w_conv_out, w_out, g_xa, g_mem, w_xq, w_xk, w_xv, w_xo,
                   g_ffn, w_rg, b_rg, w_re, b_re, w_eg, w_eu, w_ed):
    row = lambda a: a[l].reshape(1, -1)
    wp = jnp.zeros((POOL_WIDTH, D_MODEL), F32)
    gout = D_MODEL // len(POOL_WINDOWS)
    for g in range(len(POOL_WINDOWS)):
        wp = wp.at[g * POOL_GDIM:(g + 1) * POOL_GDIM, g * gout:(g + 1) * gout].set(w_pool[l, g])
    wr = jnp.zeros((D_MODEL, ROUTER_LANES), F32)
    wr = wr.at[:, :N_GROUPS].set(w_rg[l]).at[:, LANE_E0:LANE_E0 + N_EXPERTS].set(w_re[l])
    wr_hi = wr.astype(BF16)
    br = jnp.zeros((1, ROUTER_LANES), F32)
    br = br.at[0, :N_GROUPS].set(b_rg[l]).at[0, LANE_E0:LANE_E0 + N_EXPERTS].set(b_re[l])
    return {
        'g_mix': row(g_mix), 'w_in': w_in[l].astype(BF16), 'b_gate': b_gate[l],
        'w_pool_bd': wp.astype(BF16), 's_pool': row(s_pool), 'g_gv': row(g_gv), 'b_gv': row(b_gv),
        'w_s': w_s[l], 'b_s_t': jnp.transpose(b_s[l]), 'w_gmlp_out': w_gmlp_out[l].astype(BF16),
        'w_dw': w_dw[l], 'b_dw': row(b_dw), 'g_cln': row(g_cln), 'b_cln': row(b_cln),
        'w_conv_out': w_conv_out[l].astype(BF16), 'w_out': w_out[l].astype(BF16),
        'g_xa': row(g_xa), 'g_mem': row(g_mem), 'w_xq': (w_xq[l] * (XA_HDIM ** -0.5)).astype(BF16),
        'w_xk': w_xk[l].astype(BF16), 'w_xv': w_xv[l].astype(BF16), 'w_xo': w_xo[l].astype(BF16),
        'g_ffn': row(g_ffn), 'wr_hi': wr_hi, 'b_r': br,
        'wr_cat': jnp.concatenate([wr_hi, (wr - wr_hi.astype(F32)).astype(BF16)], axis=1),
        'w_eg': w_eg[l].astype(BF16), 'w_eu': w_eu[l].astype(BF16), 'w_ed': w_ed[l].astype(BF16),
    }


def _attn_moe(x, k, v, w, g_final, *, tm, final_norm):
    n = x.shape[0] * x.shape[1]
    if n >= MOE_SORTED_MIN_ROWS:
        x, bucket = _cross_attn(x, k, v, w, tm=tm, route=True)
        return _moe_sorted(x, bucket, w, g_final, tmg=MOE_SORTED_TILE, final_norm=final_norm)
    x = _cross_attn(x, k, v, w, tm=tm, route=False)
    return _moe_dense(x, w, g_final, tm=_token_tile(n), final_norm=final_norm)


def _pad_hist(h, rows):
    return jnp.pad(h, ((0, 0), (rows - h.shape[1], 0), (0, 0)))


def _token_tile(t):
    return min(1024, t)


def kernel(x_prompt, x_sample, mem_prompt, cache_pool, cache_conv, cache_mem_k, cache_mem_v, g_mix, w_in, b_gate, w_pool, s_pool, g_gv, b_gv, w_s, b_s, w_gmlp_out, w_dw, b_dw, g_cln, b_cln, w_conv_out, w_out, g_xa, g_mem, w_xq, w_xk, w_xv, w_xo, g_ffn, w_rg, b_rg, w_re, b_re, w_eg, w_eu, w_ed, g_final):
    depth = w_in.shape[0]
    bp, tp, _ = x_prompt.shape
    bs, ts, _ = x_sample.shape
    tmp, tms = _token_tile(tp), _token_tile(ts)
    gf = g_final.reshape(1, -1)
    xp, xs = x_prompt, x_sample
    pool0 = jnp.zeros((bp, POOL_HIST_PAD, POOL_WIDTH), F32)
    conv0 = jnp.zeros((bp, CONV_HIST_PAD, CONV_WIDTH), F32)
    outs = {k: [] for k in ('pp', 'pc', 'pk', 'pv', 'sp', 'sc', 'sv')}
    for l in range(depth):
        w = _layer_weights(l, g_mix, w_in, b_gate, w_pool, s_pool, g_gv, b_gv, w_s, b_s, w_gmlp_out,
                           w_dw, b_dw, g_cln, b_cln, w_conv_out, w_out, g_xa, g_mem, w_xq, w_xk, w_xv,
                           w_xo, g_ffn, w_rg, b_rg, w_re, b_re, w_eg, w_eu, w_ed)
        last = l == depth - 1
        xp, ph, ch = _mixer(xp, pool0, conv0, w, tm=tmp, pos0=0, emit_vn=False)
        mk, mv = _memory_kv(mem_prompt, w, tm=_token_tile(bp * MEM_LEN))
        xp = _attn_moe(xp, mk, mv, w, gf, tm=tmp, final_norm=last)
        outs['pp'].append(ph[:, POOL_HIST_PAD - POOL_HIST:])
        outs['pc'].append(ch[:, CONV_HIST_PAD - CONV_HIST:])
        outs['pk'].append(mk.reshape(bp, MEM_LEN, XA_HEADS, XA_HDIM))
        outs['pv'].append(mv.reshape(bp, MEM_LEN, XA_HEADS, XA_HDIM))
        xs, sh, sc, vn = _mixer(xs, _pad_hist(cache_pool[l], POOL_HIST_PAD),
                                _pad_hist(cache_conv[l], CONV_HIST_PAD), w,
                                tm=tms, pos0=PAST_LEN, emit_vn=True)
        xs = _attn_moe(xs, cache_mem_k[l].reshape(bs, MEM_LEN, D_MODEL),
                       cache_mem_v[l].reshape(bs, MEM_LEN, D_MODEL), w, gf, tm=tms, final_norm=last)
        outs['sp'].append(sh[:, POOL_HIST_PAD - POOL_HIST:])
        outs['sc'].append(sc[:, CONV_HIST_PAD - CONV_HIST:])
        outs['sv'].append(vn)
    st = lambda k: jnp.stack(outs[k], axis=0)
    return (xp, xs, st('pp'), st('pc'), st('pk'), st('pv'), st('sp'), st('sc'), st('sv'))
```
